```python
import math
import jax, jax.numpy as jnp
from jax import lax
import numpy as np

D_MODEL = 4096
BATCH = 1
SEQ = 8192
DEPTH = 1
DEC_BATCH = 128
DEC_SEQ = 4
PAST_LEN = 2048
PAGE_SIZE = 128

HEAD_DIM = 128
N_HEADS_A = (D_MODEL // 2) // HEAD_DIM
N_HEADS_B = (D_MODEL // 2) // (2 * HEAD_DIM)
MIX_WIDTH = N_HEADS_A * HEAD_DIM + N_HEADS_B * 2 * HEAD_DIM
N_IDX_HEADS = 32
IDX_DIM = 128
TOPK_MAX = 256
D_FF = 256 * ((8 * D_MODEL // 3 + 255) // 256)
CONV_W = 3
QBLK = 128
ROPE_THETA = 10000.0
EPS = 1e-6
HEAD_SCALE = HEAD_DIM ** -0.5
IDX_SCALE = IDX_DIM ** -0.5
IDX_HEAD_SCALE = N_IDX_HEADS ** -0.5
IN_SIZES = (N_HEADS_A * HEAD_DIM,) * 3 + (N_IDX_HEADS * IDX_DIM, IDX_DIM, N_IDX_HEADS) + (N_HEADS_B * 2 * HEAD_DIM,) * 3
IN_WIDTH = sum(IN_SIZES)

kernel_name = 'hybrid_dsa_diffattn_convffn_adaln_step'


def rmsnorm(x, g):
    xf = x.astype(jnp.float32)
    y = xf * lax.rsqrt(jnp.mean(xf * xf, axis=-1, keepdims=True) + EPS)
    return (y * g.astype(jnp.float32)).astype(x.dtype)


def rope(x, pos):
    d = x.shape[-1]
    inv = ROPE_THETA ** (-jnp.arange(0, d, 2, dtype=jnp.float32) / d)
    ang = pos.astype(jnp.float32)[:, None] * inv[None, :]
    shape = (1, pos.shape[0]) + (1,) * (x.ndim - 3) + (d // 2,)
    cos = jnp.cos(ang).reshape(shape).astype(x.dtype)
    sin = jnp.sin(ang).reshape(shape).astype(x.dtype)
    x1, x2 = x[..., : d // 2], x[..., d // 2:]
    return jnp.concatenate([x1 * cos - x2 * sin, x1 * sin + x2 * cos], axis=-1)


def project(h, pos, w_in):
    B, T, _ = h.shape
    offs = np.cumsum(IN_SIZES)[:-1].tolist()
    qa, ka, va, iq, ik, iw, qb, kb, vb = jnp.split(h @ w_in, offs, axis=-1)
    return {
        'qa': rope(qa.reshape(B, T, N_HEADS_A, HEAD_DIM), pos),
        'ka': rope(ka.reshape(B, T, N_HEADS_A, HEAD_DIM), pos),
        'va': va.reshape(B, T, N_HEADS_A, HEAD_DIM),
        'iq': rope(iq.reshape(B, T, N_IDX_HEADS, IDX_DIM), pos),
        'ik': rope(ik, pos),
        'iw': iw * IDX_HEAD_SCALE,
        'qb': rope(qb.reshape(B, T, N_HEADS_B, 2, HEAD_DIM), pos),
        'kb': rope(kb.reshape(B, T, N_HEADS_B, 2, HEAD_DIM), pos),
        'vb': vb.reshape(B, T, N_HEADS_B, 2 * HEAD_DIM),
    }


def indexer_topk(iq, ik, iw, admissible, n_sel):
    sc = jnp.einsum('...qhd,...sd->...qhs', iq, ik).astype(jnp.float32) * IDX_SCALE
    score = jnp.einsum('...qhs,...qh->...qs', jax.nn.relu(sc), iw.astype(jnp.float32))
    score = jnp.where(admissible, score, -jnp.inf)
    return lax.top_k(score, n_sel)[1]


def sparse_attend(q, ks, vs, valid):
    s = jnp.einsum('...qhd,...qkhd->...qhk', q, ks).astype(jnp.float32) * HEAD_SCALE
    s = jnp.where(valid[..., :, None, :], s, -jnp.inf)
    p = jax.nn.softmax(s, axis=-1).astype(vs.dtype)
    return jnp.einsum('...qhk,...qkhd->...qhd', p, vs)


def diff_attend(q, k, v, mask, lam):
    s = jnp.einsum('...qhcd,...khcd->...hcqk', q, k).astype(jnp.float32) * HEAD_SCALE
    p = jax.nn.softmax(jnp.where(mask, s, -jnp.inf), axis=-1)
    a = (p[..., 0, :, :] - lam * p[..., 1, :, :]).astype(v.dtype)
    return jnp.einsum('...hqk,...khe->...qhe', a, v)


def prompt_attend(pr, lam):
    B, T = pr['qa'].shape[:2]
    n_sel = min(TOPK_MAX, T // 4)
    key_pos = jnp.arange(T)
    gather = jax.vmap(lambda rows, i: rows[i])

    def block(q0):
        sl = lambda a: lax.dynamic_slice_in_dim(a, q0, QBLK, axis=1)
        qpos = q0 + jnp.arange(QBLK)
        admissible = key_pos[None, :] <= qpos[:, None]
        idx = indexer_topk(sl(pr['iq']), pr['ik'], sl(pr['iw']), admissible, n_sel)
        oa = sparse_attend(sl(pr['qa']), gather(pr['ka'], idx), gather(pr['va'], idx), idx <= qpos[:, None])
        ob = diff_attend(sl(pr['qb']), pr['kb'], pr['vb'], admissible, lam)
        return oa, ob

    oa, ob = lax.map(block, jnp.arange(0, T, QBLK))
    merge = lambda o: jnp.moveaxis(o, 0, 1).reshape((B, T) + o.shape[3:])
    return merge(oa), merge(ob)


def sample_attend(pr, page_table, ck_a, cv_a, ck_idx, ck_b, cv_b, layer, lam):
    S = pr['qa'].shape[1]
    L = PAST_LEN + S
    n_sel = min(TOPK_MAX, L // 4)
    qpos = PAST_LEN + jnp.arange(S)
    admissible = jnp.arange(L)[None, :] <= qpos[:, None]

    def one(a):
        pt = a['pt']
        ik_all = jnp.concatenate([ck_idx[layer, pt].reshape(PAST_LEN, IDX_DIM), a['ik']], axis=0)
        idx = indexer_topk(a['iq'], ik_all, a['iw'], admissible, n_sel)
        in_past = (idx < PAST_LEN)[..., None, None]
        pidx = jnp.minimum(idx, PAST_LEN - 1)
        page, off = pt[pidx // PAGE_SIZE], pidx % PAGE_SIZE
        nidx = jnp.clip(idx - PAST_LEN, 0, S - 1)
        ks = jnp.where(in_past, ck_a[layer, page, off], a['ka'][nidx])
        vs = jnp.where(in_past, cv_a[layer, page, off], a['va'][nidx])
        oa = sparse_attend(a['qa'], ks, vs, idx <= qpos[:, None])
        kb = jnp.concatenate([ck_b[layer, pt].reshape(PAST_LEN, N_HEADS_B, 2, HEAD_DIM), a['kb']], axis=0)
        vb = jnp.concatenate([cv_b[layer, pt].reshape(PAST_LEN, N_HEADS_B, 2 * HEAD_DIM), a['vb']], axis=0)
        ob = diff_attend(a['qb'], kb, vb, admissible, lam)
        return oa, ob

    return lax.map(one, dict(pr, pt=page_table))


def decoder_layer(x, c, pos, conv_prev, attend, lam_init, w_ada, b_ada, g_attn, w_in, g_subln, w_o,
                  g_ffn, w_up, w_conv, b_conv, w_down):
    B, T, _ = x.shape
    mod = (c @ w_ada + b_ada)[:, None, :]
    sh1, sc1, gt1, sh2, sc2, gt2 = jnp.split(mod, 6, axis=-1)
    h = rmsnorm(x, g_attn) * (1 + sc1) + sh1
    pr = project(h, pos, w_in)
    oa, ob = attend(pr)
    ob = rmsnorm(ob, g_subln) * (1 - lam_init)
    o = jnp.concatenate([oa.reshape(B, T, -1), ob.reshape(B, T, -1)], axis=-1)
    x = x + gt1 * (o @ w_o)
    h2 = rmsnorm(x, g_ffn) * (1 + sc2) + sh2
    g, v = jnp.split(h2 @ w_up, 2, axis=-1)
    g_cat = jnp.concatenate([conv_prev, g], axis=1)
    g_conv = b_conv + sum(w_conv[j] * g_cat[:, j:j + T] for j in range(CONV_W))
    x = x + gt2 * ((jax.nn.silu(g_conv) * v) @ w_down)
    return x, (pr['ka'], pr['va'], pr['ik'], pr['kb'], pr['vb'], g_cat[:, T:])


def setup_inputs(seed: int = 0) -> dict:
    key = jax.random.key(seed)
    k = jax.random.split(key, 32)
    n_pages = PAST_LEN // PAGE_SIZE
    n_used = DEC_BATCH * n_pages
    n_phys = n_used + (n_used + 3) // 4
    nrm = lambda kk, shape, s: jax.random.normal(kk, shape, jnp.float32) * s
    page_table = jax.random.permutation(k[0], n_phys)[:n_used].reshape(DEC_BATCH, n_pages).astype(jnp.int32)
    return {
        'x_prompt': nrm(k[1], (BATCH, SEQ, D_MODEL), 1.0),
        'x_sample': nrm(k[2], (DEC_BATCH, DEC_SEQ, D_MODEL), 1.0),
        'cache_k_a': nrm(k[3], (DEPTH, n_phys, PAGE_SIZE, N_HEADS_A, HEAD_DIM), 1.0),
        'cache_v_a': nrm(k[4], (DEPTH, n_phys, PAGE_SIZE, N_HEADS_A, HEAD_DIM), 1.0),
        'cache_k_idx': nrm(k[5], (DEPTH, n_phys, PAGE_SIZE, IDX_DIM), 1.0),
        'cache_k_b': nrm(k[6], (DEPTH, n_phys, PAGE_SIZE, N_HEADS_B, 2, HEAD_DIM), 1.0),
        'cache_v_b': nrm(k[7], (DEPTH, n_phys, PAGE_SIZE, N_HEADS_B, 2 * HEAD_DIM), 1.0),
        'state_conv': nrm(k[8], (DEPTH, DEC_BATCH, CONV_W - 1, D_FF), 1.0),
        'page_table': page_table,
        'c_prompt': nrm(k[9], (BATCH, D_MODEL), 1.0),
        'c_sample': nrm(k[10], (DEC_BATCH, D_MODEL), 1.0),
        'w_ada': nrm(k[11], (DEPTH, D_MODEL, 6 * D_MODEL), 0.5 * D_MODEL ** -0.5),
        'b_ada': nrm(k[12], (DEPTH, 6 * D_MODEL), 0.01),
        'g_attn': 1.0 + nrm(k[13], (DEPTH, D_MODEL), 0.01),
        'w_in': nrm(k[14], (DEPTH, D_MODEL, IN_WIDTH), D_MODEL ** -0.5),
        'lam_q1': nrm(k[15], (DEPTH, HEAD_DIM), 0.1),
        'lam_k1': nrm(k[16], (DEPTH, HEAD_DIM), 0.1),
        'lam_q2': nrm(k[17], (DEPTH, HEAD_DIM), 0.1),
        'lam_k2': nrm(k[18], (DEPTH, HEAD_DIM), 0.1),
        'g_subln': 1.0 + nrm(k[19], (DEPTH, 2 * HEAD_DIM), 0.01),
        'w_o': nrm(k[20], (DEPTH, MIX_WIDTH, D_MODEL), MIX_WIDTH ** -0.5),
        'g_ffn': 1.0 + nrm(k[21], (DEPTH, D_MODEL), 0.01),
        'w_up': nrm(k[22], (DEPTH, D_MODEL, 2 * D_FF), D_MODEL ** -0.5),
        'w_conv': nrm(k[23], (DEPTH, CONV_W, D_FF), CONV_W ** -0.5),
        'b_conv': nrm(k[24], (DEPTH, D_FF), 0.01),
        'w_down': nrm(k[25], (DEPTH, D_FF, D_MODEL), D_FF ** -0.5),
        'g_final': 1.0 + nrm(k[26], (D_MODEL,), 0.01),
    }


def reference(x_prompt, x_sample, cache_k_a, cache_v_a, cache_k_idx, cache_k_b, cache_v_b, state_conv,
              page_table, c_prompt, c_sample, w_ada, b_ada, g_attn, w_in, lam_q1, lam_k1, lam_q2, lam_k2,
              g_subln, w_o, g_ffn, w_up, w_conv, b_conv, w_down, g_final):
    S = x_sample.shape[1]
    pos_p = jnp.arange(x_prompt.shape[1])
    pos_s = PAST_LEN + jnp.arange(S)
    conv_zero = jnp.zeros((x_prompt.shape[0], CONV_W - 1, D_FF), x_prompt.dtype)
    xp, xs = x_prompt, x_sample
    new_p, new_s = [], []
    for l in range(DEPTH):
        lam_init = 0.8 - 0.6 * math.exp(-0.3 * l)
        f32 = lambda a: a.astype(jnp.float32)
        lam = (jnp.exp(jnp.sum(f32(lam_q1[l]) * f32(lam_k1[l])))
               - jnp.exp(jnp.sum(f32(lam_q2[l]) * f32(lam_k2[l]))) + lam_init)
        lw = (w_ada[l], b_ada[l], g_attn[l], w_in[l], g_subln[l], w_o[l], g_ffn[l], w_up[l],
              w_conv[l], b_conv[l], w_down[l])
        xp, sp = decoder_layer(xp, c_prompt, pos_p, conv_zero,
                               lambda pr: prompt_attend(pr, lam), lam_init, *lw)
        xs, ss = decoder_layer(xs, c_sample, pos_s, state_conv[l],
                               lambda pr: sample_attend(pr, page_table, cache_k_a, cache_v_a, cache_k_idx,
                                                        cache_k_b, cache_v_b, l, lam), lam_init, *lw)
        new_p.append(sp)
        new_s.append(ss)
    y_prompt = rmsnorm(xp, g_final)
    y_sample = rmsnorm(xs, g_final)
    st = lambda states, i: jnp.stack([s[i] for s in states], axis=0)
    return (y_prompt, y_sample,
            st(new_p, 0), st(new_p, 1), st(new_p, 2), st(new_p, 3), st(new_p, 4), st(new_p, 5),
            st(new_s, 0), st(new_s, 1), st(new_s, 2), st(new_s, 3), st(new_s, 4), st(new_s, 5))
```

```python
import functools
import math

import jax
import jax.numpy as jnp
from jax import lax
from jax.experimental import pallas as pl
from jax.experimental.pallas import tpu as pltpu

F32 = jnp.float32
BF16 = jnp.bfloat16

HEAD_DIM = 128
IDX_DIM = 128
PAGE_SIZE = 128
TOPK_MAX = 256
CONV_W = 3
ROPE_THETA = 10000.0
EPS = 1e-6
HEAD_SCALE = HEAD_DIM ** -0.5
IDX_SCALE = IDX_DIM ** -0.5
LANES = 128
NEG = -1e30
INT_MIN = -(2 ** 31)
KEY_NEG_INF = INT_MIN + 0x7FFFFF
VMEM_LIMIT = 56 * 1024 * 1024


def _cparams(sem):
    return pltpu.CompilerParams(dimension_semantics=sem, vmem_limit_bytes=VMEM_LIMIT)


def _pick(n, cands):
    for c in cands:
        if n % c == 0:
            return c
    return n


def _ada_kernel(c_ref, w_ref, b_ref, o_ref):
    acc = jnp.dot(c_ref[...], w_ref[...].astype(BF16), preferred_element_type=F32)
    o_ref[...] = acc + b_ref[...]


def _ada(c_all, w_ada, b_ada):
    r, d = c_all.shape
    n = w_ada.shape[1]
    tn = _pick(n, (512, 256, 128))
    return pl.pallas_call(
        _ada_kernel,
        grid=(n // tn,),
        in_specs=[pl.BlockSpec((r, d), lambda j: (0, 0)),
                  pl.BlockSpec((d, tn), lambda j: (0, j)),
                  pl.BlockSpec((1, tn), lambda j: (0, j))],
        out_specs=pl.BlockSpec((r, tn), lambda j: (0, j)),
        out_shape=jax.ShapeDtypeStruct((r, n), F32),
        compiler_params=_cparams(("arbitrary",)),
        name="ada",
    )(c_all, w_ada, b_ada.reshape(1, n))


def _norm_mod_kernel(x_ref, g_ref, sc_ref, sh_ref, o_ref):
    x = x_ref[...]
    y = x * lax.rsqrt(jnp.mean(x * x, axis=-1, keepdims=True) + EPS) * g_ref[...]
    o_ref[...] = (y * (1.0 + sc_ref[...]) + sh_ref[...]).astype(o_ref.dtype)


def _norm_mod(x, g, mod, sc_blk, sh_blk, tm):
    rows, d = x.shape
    mr = mod.shape[0]
    assert mr == 1 or mr == tm
    return pl.pallas_call(
        _norm_mod_kernel,
        grid=(rows // tm,),
        in_specs=[pl.BlockSpec((tm, d), lambda i: (i, 0)),
                  pl.BlockSpec((1, d), lambda i: (0, 0)),
                  pl.BlockSpec((mr, d), lambda i: (0, sc_blk)),
                  pl.BlockSpec((mr, d), lambda i: (0, sh_blk))],
        out_specs=pl.BlockSpec((tm, d), lambda i: (i, 0)),
        out_shape=jax.ShapeDtypeStruct((rows, d), BF16),
        compiler_params=_cparams(("arbitrary",)),
        name="norm_mod",
    )(x, g.reshape(1, d), mod, mod)


def _final_norm_kernel(x_ref, g_ref, o_ref):
    x = x_ref[...]
    o_ref[...] = x * lax.rsqrt(jnp.mean(x * x, axis=-1, keepdims=True) + EPS) * g_ref[...]


def _final_norm(x, g, tm):
    rows, d = x.shape
    return pl.pallas_call(
        _final_norm_kernel,
        grid=(rows // tm,),
        in_specs=[pl.BlockSpec((tm, d), lambda i: (i, 0)),
                  pl.BlockSpec((1, d), lambda i: (0, 0))],
        out_specs=pl.BlockSpec((tm, d), lambda i: (i, 0)),
        out_shape=jax.ShapeDtypeStruct((rows, d), F32),
        compiler_params=_cparams(("arbitrary",)),
        name="final_norm",
    )(x, g.reshape(1, d))


def _rope_cols(a, cos, sin):
    outs = []
    for c in range(a.shape[1] // HEAD_DIM):
        xh = a[:, c * HEAD_DIM:(c + 1) * HEAD_DIM]
        outs.append(xh * cos + pltpu.roll(xh, HEAD_DIM // 2, 1) * sin)
    return outs[0] if len(outs) == 1 else jnp.concatenate(outs, axis=1)


def _proj_kernel(h_ref, w_ref, cos_ref, sin_ref, o32_ref, o16_ref, *, plain_lo, plain_hi, special_j, iw_scale):
    j = pl.program_id(1)
    acc = jnp.dot(h_ref[...], w_ref[...], preferred_element_type=F32)
    is_plain = ((j >= plain_lo[0]) & (j < plain_hi[0])) | ((j >= plain_lo[1]) & (j < plain_hi[1]))
    is_special = j == special_j

    def emit(v):
        o32_ref[...] = v
        o16_ref[...] = v.astype(BF16)

    @pl.when(is_plain)
    def _():
        emit(acc)

    @pl.when(is_special)
    def _():
        parts = [_rope_cols(acc[:, :IDX_DIM], cos_ref[...], sin_ref[...]), acc[:, IDX_DIM:2 * IDX_DIM] * iw_scale]
        if acc.shape[1] > 2 * IDX_DIM:
            parts.append(acc[:, 2 * IDX_DIM:])
        emit(jnp.concatenate(parts, axis=1))

    @pl.when(jnp.logical_not(is_plain | is_special))
    def _():
        emit(_rope_cols(acc, cos_ref[...], sin_ref[...]))


def _proj(h, w_perm, cosf, sinf, lay, tm):
    rows, d = h.shape
    n = w_perm.shape[1]
    tn = lay["tn"]
    kern = functools.partial(
        _proj_kernel,
        plain_lo=(lay["va"] // tn, lay["vb"] // tn), plain_hi=(lay["qb"] // tn, lay["sp"] // tn),
        special_j=lay["sp"] // tn, iw_scale=lay["n_idx"] ** -0.5)
    return pl.pallas_call(
        kern,
        grid=(rows // tm, n // tn),
        in_specs=[pl.BlockSpec((tm, d), lambda i, j: (i, 0)),
                  pl.BlockSpec((d, tn), lambda i, j: (0, j)),
                  pl.BlockSpec((tm, HEAD_DIM), lambda i, j: (i, 0)),
                  pl.BlockSpec((tm, HEAD_DIM), lambda i, j: (i, 0))],
        out_specs=[pl.BlockSpec((tm, tn), lambda i, j: (i, j)),
                   pl.BlockSpec((tm, tn), lambda i, j: (i, j))],
        out_shape=[jax.ShapeDtypeStruct((rows, n), F32), jax.ShapeDtypeStruct((rows, n), BF16)],
        compiler_params=_cparams(("arbitrary", "arbitrary")),
        name="proj",
    )(h, w_perm, cosf, sinf)


def _float_key(x):
    bits = pltpu.bitcast(x, jnp.int32)
    return bits ^ ((bits >> 31) & 0x7FFFFFFF)


def _topk_bias(key_ref, thr_ref, o_ref, *, rows, nch, total_ch, k, rg):
    for g in range(rows // rg):
        r0 = g * rg

        def bit_body(b, t, r0=r0):
            cand = t + lax.shift_left(jnp.int32(1), jnp.int32(31) - b)

            def ch_body(c, cnt):
                kk = key_ref[r0:r0 + rg, pl.ds(pl.multiple_of(c * LANES, LANES), LANES)]
                return cnt + jnp.where(kk >= cand, 1.0, 0.0)

            cnt = lax.fori_loop(0, nch, ch_body, jnp.zeros((rg, LANES), F32))
            tot = jnp.sum(cnt, axis=1, keepdims=True)
            return jnp.where(tot >= float(k), cand, t)

        t = lax.fori_loop(0, 32, bit_body, jnp.full((rg, LANES), INT_MIN, jnp.int32))
        thr_ref[r0:r0 + rg, :] = jnp.maximum(t, KEY_NEG_INF + 1)

    thr = thr_ref[...]

    def out_body(c, carry):
        sl = pl.ds(pl.multiple_of(c * LANES, LANES), LANES)
        o_ref[:, sl] = jnp.where(key_ref[:, sl] >= thr, 0.0, NEG).astype(o_ref.dtype)
        return carry

    def fill_body(c, carry):
        sl = pl.ds(pl.multiple_of(c * LANES, LANES), LANES)
        o_ref[:, sl] = jnp.full((rows, LANES), NEG, o_ref.dtype)
        return carry

    lax.fori_loop(0, nch, out_body, 0)
    lax.fori_loop(nch, total_ch, fill_body, 0)


def _score_p_kernel(iq_ref, ik_ref, iw_ref, o_ref, key_scr, wb_scr, thr_scr, *, tq, tk, n_idx, k_sel, nj, t_len):
    i = pl.program_id(0)
    j = pl.program_id(1)
    jmax = ((i + 1) * tq - 1) // tk

    @pl.when(j == 0)
    def _():
        w = iw_ref[...] * IDX_SCALE
        for h in range(n_idx):
            wb_scr[h] = jnp.broadcast_to(w[:, h:h + 1], (tq, LANES))

    @pl.when(j <= jmax)
    def _():
        kb = ik_ref[...]
        acc = jnp.zeros((tq, tk), F32)
        for h in range(n_idx):
            s = lax.dot_general(iq_ref[:, h * IDX_DIM:(h + 1) * IDX_DIM], kb, (((1,), (1,)), ((), ())),
                                preferred_element_type=F32)
            acc = acc + jnp.maximum(s, 0.0) * pltpu.repeat(wb_scr[h], tk // LANES, axis=1)
        rows = i * tq + lax.broadcasted_iota(jnp.int32, (tq, tk), 0)
        cols = j * tk + lax.broadcasted_iota(jnp.int32, (tq, tk), 1)
        acc = jnp.where(cols <= rows, acc, -jnp.inf)
        key_scr[:, pl.ds(pl.multiple_of(j * tk, tk), tk)] = _float_key(acc)

    @pl.when(j == nj - 1)
    def _():
        _topk_bias(key_scr, thr_scr, o_ref, rows=tq, nch=(jmax + 1) * (tk // LANES), total_ch=t_len // LANES,
                   k=k_sel, rg=min(tq, 64))


def _score_p(p16, p32, lay, tq, tk, k_sel):
    t_len = p16.shape[0]
    n_idx = lay["n_idx"]
    qi = n_idx * IDX_DIM
    ni, nj = t_len // tq, t_len // tk
    sp_blk = lay["sp"] // IDX_DIM
    kern = functools.partial(_score_p_kernel, tq=tq, tk=tk, n_idx=n_idx, k_sel=k_sel, nj=nj, t_len=t_len)
    return pl.pallas_call(
        kern,
        grid=(ni, nj),
        in_specs=[pl.BlockSpec((tq, qi), lambda i, j: (i, 0)),
                  pl.BlockSpec((tk, IDX_DIM), lambda i, j: (jnp.minimum(j, ((i + 1) * tq - 1) // tk), sp_blk)),
                  pl.BlockSpec((tq, IDX_DIM), lambda i, j: (i, sp_blk + 1))],
        out_specs=pl.BlockSpec((tq, t_len), lambda i, j: (i, 0)),
        out_shape=jax.ShapeDtypeStruct((t_len, t_len), BF16),
        scratch_shapes=[pltpu.VMEM((tq, t_len), jnp.int32),
                        pltpu.VMEM((n_idx, tq, LANES), F32),
                        pltpu.VMEM((tq, LANES), jnp.int32)],
        compiler_params=_cparams(("arbitrary", "arbitrary")),
        name="score_prompt",
    )(p16, p16, p32)


def _thresh_kernel(s_ref, o_ref, key_scr, thr_scr, *, rows, nch, k_sel):
    key_scr[...] = _float_key(s_ref[...])
    _topk_bias(key_scr, thr_scr, o_ref, rows=rows, nch=nch, total_ch=nch, k=k_sel, rg=min(rows, 64))


def _thresh(s, k_sel):
    rows, l = s.shape
    kern = functools.partial(_thresh_kernel, rows=rows, nch=l // LANES, k_sel=k_sel)
    return pl.pallas_call(
        kern,
        grid=(1,),
        in_specs=[pl.BlockSpec((rows, l), lambda i: (0, 0))],
        out_specs=pl.BlockSpec((rows, l), lambda i: (0, 0)),
        out_shape=jax.ShapeDtypeStruct((rows, l), F32),
        scratch_shapes=[pltpu.VMEM((rows, l), jnp.int32), pltpu.VMEM((rows, LANES), jnp.int32)],
        compiler_params=_cparams(("arbitrary",)),
        name="thresh_sample",
    )(s)


def _online_step(s, v, m_scr, l_scr, hd, acc_ref, c0, c1):
    tk = s.shape[1]
    m_prev = m_scr[hd]
    l_prev = l_scr[hd]
    m_new = jnp.maximum(m_prev, jnp.max(s, axis=1, keepdims=True))
    alpha = jnp.exp(m_prev - m_new)
    p = jnp.exp(s - pltpu.repeat(m_new, tk // LANES, axis=1))
    l_scr[hd] = alpha * l_prev + jnp.sum(p, axis=1, keepdims=True)
    m_scr[hd] = m_new
    pv = jnp.dot(p.astype(BF16), v, preferred_element_type=F32)
    arep = alpha if (c1 - c0) == LANES else pltpu.repeat(alpha, (c1 - c0) // LANES, axis=1)
    acc_ref[:, c0:c1] = acc_ref[:, c0:c1] * arep + pv


def _qk(q, k):
    return lax.dot_general(q, k, (((1,), (1,)), ((), ())), preferred_element_type=F32)


def _sattn_p_kernel(q_ref, k_ref, v_ref, b_ref, o_ref, m_scr, l_scr, acc_scr, *, tq, tk, nh, nj):
    i = pl.program_id(0)
    j = pl.program_id(1)
    jmax = ((i + 1) * tq - 1) // tk

    @pl.when(j == 0)
    def _():
        m_scr[...] = jnp.full(m_scr.shape, NEG, F32)
        l_scr[...] = jnp.zeros(l_scr.shape, F32)
        acc_scr[...] = jnp.zeros(acc_scr.shape, F32)

    @pl.when(j <= jmax)
    def _():
        bias = b_ref[...].astype(F32)
        for h in range(nh):
            c0, c1 = h * HEAD_DIM, (h + 1) * HEAD_DIM
            s = _qk(q_ref[:, c0:c1], k_ref[:, c0:c1]) * HEAD_SCALE + bias
            _online_step(s, v_ref[:, c0:c1], m_scr, l_scr, h, acc_scr, c0, c1)

    @pl.when(j == nj - 1)
    def _():
        for h in range(nh):
            c0, c1 = h * HEAD_DIM, (h + 1) * HEAD_DIM
            o_ref[:, c0:c1] = (acc_scr[:, c0:c1] / l_scr[h]).astype(o_ref.dtype)


def _sattn_p(p16, bias, lay, tq, tk):
    t_len = p16.shape[0]
    w2 = lay["w2"]
    nh = w2 // HEAD_DIM
    ni, nj = t_len // tq, t_len // tk
    jm = lambda i, j: jnp.minimum(j, ((i + 1) * tq - 1) // tk)
    kern = functools.partial(_sattn_p_kernel, tq=tq, tk=tk, nh=nh, nj=nj)
    return pl.pallas_call(
        kern,
        grid=(ni, nj),
        in_specs=[pl.BlockSpec((tq, w2), lambda i, j: (i, lay["qa"] // w2)),
                  pl.BlockSpec((tk, w2), lambda i, j: (jm(i, j), lay["ka"] // w2)),
                  pl.BlockSpec((tk, w2), lambda i, j: (jm(i, j), lay["va"] // w2)),
                  pl.BlockSpec((tq, tk), lambda i, j: (i, jm(i, j)))],
        out_specs=pl.BlockSpec((tq, w2), lambda i, j: (i, 0)),
        out_shape=jax.ShapeDtypeStruct((t_len, w2), BF16),
        scratch_shapes=[pltpu.VMEM((nh, tq, LANES), F32), pltpu.VMEM((nh, tq, LANES), F32),
                        pltpu.VMEM((tq, w2), F32)],
        compiler_params=_cparams(("arbitrary", "arbitrary")),
        name="sparse_attn_prompt",
    )(p16, p16, p16, bias)


def _lam_value(lq1, lk1, lq2, lk2, lam_init):
    a = jnp.sum(lq1[...] * lk1[...], axis=1, keepdims=True)
    b = jnp.sum(lq2[...] * lk2[...], axis=1, keepdims=True)
    return jnp.exp(a) - jnp.exp(b) + lam_init


def _subln(o, g, lam_init):
    return o * lax.rsqrt(jnp.mean(o * o, axis=1, keepdims=True) + EPS) * g * (1.0 - lam_init)


def _dattn_p_kernel(q_ref, k_ref, v_ref, lq1, lk1, lq2, lk2, g_ref, o_ref, m_scr, l_scr, acc0_scr, acc1_scr,
                    *, tq, tk, nhb, nj, lam_init):
    i = pl.program_id(0)
    j = pl.program_id(1)
    jmax = ((i + 1) * tq - 1) // tk
    dv = 2 * HEAD_DIM

    @pl.when(j == 0)
    def _():
        m_scr[...] = jnp.full(m_scr.shape, NEG, F32)
        l_scr[...] = jnp.zeros(l_scr.shape, F32)
        acc0_scr[...] = jnp.zeros(acc0_scr.shape, F32)
        acc1_scr[...] = jnp.zeros(acc1_scr.shape, F32)

    @pl.when(j <= jmax)
    def _():
        rows = i * tq + lax.broadcasted_iota(jnp.int32, (tq, tk), 0)
        cols = j * tk + lax.broadcasted_iota(jnp.int32, (tq, tk), 1)
        bias = jnp.where(cols <= rows, 0.0, NEG)
        for h in range(nhb):
            v = v_ref[:, h * dv:(h + 1) * dv]
            for c, acc in ((0, acc0_scr), (1, acc1_scr)):
                hd = 2 * h + c
                c0, c1 = hd * HEAD_DIM, (hd + 1) * HEAD_DIM
                s = _qk(q_ref[:, c0:c1], k_ref[:, c0:c1]) * HEAD_SCALE + bias
                _online_step(s, v, m_scr, l_scr, hd, acc, h * dv, (h + 1) * dv)

    @pl.when(j == nj - 1)
    def _():
        lam = _lam_value(lq1, lk1, lq2, lk2, lam_init)
        g = g_ref[...]
        for h in range(nhb):
            c0, c1 = h * dv, (h + 1) * dv
            o0 = acc0_scr[:, c0:c1] / pltpu.repeat(l_scr[2 * h], 2, axis=1)
            o1 = acc1_scr[:, c0:c1] / pltpu.repeat(l_scr[2 * h + 1], 2, axis=1)
            o_ref[:, c0:c1] = _subln(o0 - lam * o1, g, lam_init).astype(o_ref.dtype)


def _dattn_p(p16, lams, g_subln, lay, tq, tk, lam_init):
    t_len = p16.shape[0]
    w2 = lay["w2"]
    nhb = w2 // (2 * HEAD_DIM)
    ni, nj = t_len // tq, t_len // tk
    jm = lambda i, j: jnp.minimum(j, ((i + 1) * tq - 1) // tk)
    kern = functools.partial(_dattn_p_kernel, tq=tq, tk=tk, nhb=nhb, nj=nj, lam_init=lam_init)
    vec = pl.BlockSpec((1, HEAD_DIM), lambda i, j: (0, 0))
    return pl.pallas_call(
        kern,
        grid=(ni, nj),
        in_specs=[pl.BlockSpec((tq, w2), lambda i, j: (i, lay["qb"] // w2)),
                  pl.BlockSpec((tk, w2), lambda i, j: (jm(i, j), lay["kb"] // w2)),
                  pl.BlockSpec((tk, w2), lambda i, j: (jm(i, j), lay["vb"] // w2)),
                  vec, vec, vec, vec,
                  pl.BlockSpec((1, 2 * HEAD_DIM), lambda i, j: (0, 0))],
        out_specs=pl.BlockSpec((tq, w2), lambda i, j: (i, 0)),
        out_shape=jax.ShapeDtypeStruct((t_len, w2), BF16),
        scratch_shapes=[pltpu.VMEM((2 * nhb, tq, LANES), F32), pltpu.VMEM((2 * nhb, tq, LANES), F32),
                        pltpu.VMEM((tq, w2), F32), pltpu.VMEM((tq, w2), F32)],
        compiler_params=_cparams(("arbitrary", "arbitrary")),
        name="diff_attn_prompt",
    )(p16, p16, p16, *lams, g_subln)


def _score_s_kernel(pt_ref, iq_ref, iw_ref, iknew_ref, *rest, n_pages, s_len, n_idx, past_len):
    page_refs = rest[:n_pages]
    o_ref = rest[n_pages]
    q = iq_ref[...].astype(BF16)
    w = iw_ref[...] * IDX_SCALE
    knew = jnp.concatenate([iknew_ref[...], jnp.zeros((PAGE_SIZE - 8, IDX_DIM), F32)], axis=0)
    blocks = [r[...] for r in page_refs] + [knew]
    for p, kb in enumerate(blocks):
        s = _qk(q, kb.astype(BF16))
        s = jnp.maximum(s, 0.0) * w
        sc = jnp.sum(s.reshape(s_len, n_idx, PAGE_SIZE), axis=1)
        cols = p * PAGE_SIZE + lax.broadcasted_iota(jnp.int32, (s_len, PAGE_SIZE), 1)
        qpos = past_len + lax.broadcasted_iota(jnp.int32, (s_len, PAGE_SIZE), 0)
        o_ref[:, p * PAGE_SIZE:(p + 1) * PAGE_SIZE] = jnp.where(cols <= qpos, sc, -jnp.inf)


def _score_s(page_table, iq_s, iw_s, iknew8, ck_idx, past_len, s_len):
    nb, rows, _ = iq_s.shape
    n_pages = page_table.shape[1]
    n_idx = rows // s_len
    l_pad = (n_pages + 1) * PAGE_SIZE
    kern = functools.partial(_score_s_kernel, n_pages=n_pages, s_len=s_len, n_idx=n_idx, past_len=past_len)
    page_specs = [pl.BlockSpec((None, PAGE_SIZE, IDX_DIM), functools.partial(lambda b, pt, p: (pt[b, p], 0, 0), p=p))
                  for p in range(n_pages)]
    grid_spec = pltpu.PrefetchScalarGridSpec(
        num_scalar_prefetch=1,
        grid=(nb,),
        in_specs=[pl.BlockSpec((None, rows, IDX_DIM), lambda b, pt: (b, 0, 0)),
                  pl.BlockSpec((None, rows, 1), lambda b, pt: (b, 0, 0)),
                  pl.BlockSpec((None, 8, IDX_DIM), lambda b, pt: (b, 0, 0))] + page_specs,
        out_specs=pl.BlockSpec((None, s_len, l_pad), lambda b, pt: (b, 0, 0)),
    )
    return pl.pallas_call(
        kern,
        grid_spec=grid_spec,
        out_shape=jax.ShapeDtypeStruct((nb, s_len, l_pad), F32),
        compiler_params=_cparams(("arbitrary",)),
        name="score_sample",
    )(page_table, iq_s, iw_s, iknew8, *([ck_idx] * n_pages))


def _blockdiag_q(q, n_slots, width):
    s_len, w = q.shape
    rep = jnp.broadcast_to(q[:, None, :], (s_len, n_slots, w)).reshape(s_len * n_slots, w)
    return jnp.where(_diag_mask(s_len, n_slots, w, width), rep, 0.0)


def _diag_mask(s_len, n_slots, w, width):
    r = lax.broadcasted_iota(jnp.int32, (s_len * n_slots, w), 0) % n_slots
    c = lax.broadcasted_iota(jnp.int32, (s_len * n_slots, w), 1) // width
    return r == c


def _page_step(qbd_scr, k, v, bias, m_scr, l_scr, acc_scr):
    s = _qk(qbd_scr[...], k) * HEAD_SCALE + bias
    m_prev = m_scr[...]
    m_new = jnp.maximum(m_prev, jnp.max(s, axis=1, keepdims=True))
    alpha = jnp.exp(m_prev - m_new)
    p = jnp.exp(s - m_new)
    l_scr[...] = alpha * l_scr[...] + jnp.sum(p, axis=1, keepdims=True)
    m_scr[...] = m_new
    pv = jnp.dot(p.astype(BF16), v, preferred_element_type=F32)
    acc_scr[...] = acc_scr[...] * pltpu.repeat(alpha, acc_scr.shape[1] // LANES, axis=1) + pv


def _attn_s_kernel(pt_ref, qa_ref, qb_ref, kan_ref, van_ref, kbn_ref, vbn_ref, bias_ref,
                   cka_ref, cva_ref, ckb_ref, cvb_ref, lq1, lk1, lq2, lk2, g_ref,
                   oa_ref, ob_ref,
                   qa_scr, qb_scr, ma_scr, la_scr, acca_scr, mb_scr, lb_scr, accb_scr,
                   *, n_pages, s_len, nslot, w2, lam_init):
    p = pl.program_id(1)
    rows = s_len * nslot

    @pl.when(p == 0)
    def _():
        qa_scr[...] = _blockdiag_q(qa_ref[...], nslot, HEAD_DIM).astype(BF16)
        qb_scr[...] = _blockdiag_q(qb_ref[...], nslot, HEAD_DIM).astype(BF16)
        for m_scr, l_scr, acc_scr in ((ma_scr, la_scr, acca_scr), (mb_scr, lb_scr, accb_scr)):
            m_scr[...] = jnp.full(m_scr.shape, NEG, F32)
            l_scr[...] = jnp.zeros(l_scr.shape, F32)
            acc_scr[...] = jnp.zeros(acc_scr.shape, F32)

    def bias_rows(b4):
        return jnp.broadcast_to(b4[:, None, :], (s_len, nslot, PAGE_SIZE)).reshape(rows, PAGE_SIZE)

    @pl.when(p < n_pages)
    def _():
        _page_step(qa_scr, cka_ref[...].astype(BF16), cva_ref[...].astype(BF16), bias_rows(bias_ref[...]),
                   ma_scr, la_scr, acca_scr)
        _page_step(qb_scr, ckb_ref[...].astype(BF16), cvb_ref[...].astype(BF16), 0.0,
                   mb_scr, lb_scr, accb_scr)

    @pl.when(p == n_pages)
    def _():
        pad = lambda r: jnp.concatenate([r[...], jnp.zeros((PAGE_SIZE - 8, w2), F32)], axis=0).astype(BF16)
        _page_step(qa_scr, pad(kan_ref), pad(van_ref), bias_rows(bias_ref[...]), ma_scr, la_scr, acca_scr)
        kcol = lax.broadcasted_iota(jnp.int32, (s_len, PAGE_SIZE), 1)
        qrow = lax.broadcasted_iota(jnp.int32, (s_len, PAGE_SIZE), 0)
        causal = jnp.where(kcol <= qrow, 0.0, NEG)
        _page_step(qb_scr, pad(kbn_ref), pad(vbn_ref), bias_rows(causal), mb_scr, lb_scr, accb_scr)

        oa = jnp.where(_diag_mask(s_len, nslot, w2, HEAD_DIM), acca_scr[...] / la_scr[...][:, :1], 0.0)
        oa_ref[...] = jnp.sum(oa.reshape(s_len, nslot, w2), axis=1)

        nb = accb_scr[...] / lb_scr[...][:, :1]
        r = lax.broadcasted_iota(jnp.int32, (rows, w2), 0) % nslot
        c = lax.broadcasted_iota(jnp.int32, (rows, w2), 1) // (2 * HEAD_DIM)
        own = (r // 2) == c
        o0 = jnp.sum(jnp.where(own & (r % 2 == 0), nb, 0.0).reshape(s_len, nslot, w2), axis=1)
        o1 = jnp.sum(jnp.where(own & (r % 2 == 1), nb, 0.0).reshape(s_len, nslot, w2), axis=1)
        ob = o0 - _lam_value(lq1, lk1, lq2, lk2, lam_init) * o1
        g = g_ref[...]
        dv = 2 * HEAD_DIM
        for h in range(w2 // dv):
            ob_ref[:, h * dv:(h + 1) * dv] = _subln(ob[:, h * dv:(h + 1) * dv], g, lam_init)


def _attn_s(page_table, qa, qb, kan, van, kbn, vbn, bias, cka, cva, ckb, cvb, lams, g_subln, lam_init):
    nb, s_len, w2 = qa.shape
    n_pages = page_table.shape[1]
    nslot = w2 // HEAD_DIM
    rows = s_len * nslot
    kern = functools.partial(_attn_s_kernel, n_pages=n_pages, s_len=s_len, nslot=nslot, w2=w2, lam_init=lam_init)
    seq = lambda r: pl.BlockSpec((None, r, w2), lambda b, p, pt: (b, 0, 0))
    page = pl.BlockSpec((None, PAGE_SIZE, w2), lambda b, p, pt: (pt[b, jnp.minimum(p, n_pages - 1)], 0, 0))
    vec = pl.BlockSpec((1, HEAD_DIM), lambda b, p, pt: (0, 0))
    grid_spec = pltpu.PrefetchScalarGridSpec(
        num_scalar_prefetch=1,
        grid=(nb, n_pages + 1),
        in_specs=[seq(s_len), seq(s_len), seq(8), seq(8), seq(8), seq(8),
                  pl.BlockSpec((None, s_len, PAGE_SIZE), lambda b, p, pt: (b, 0, p)),
                  page, page, page, page, vec, vec, vec, vec,
                  pl.BlockSpec((1, 2 * HEAD_DIM), lambda b, p, pt: (0, 0))],
        out_specs=[seq(s_len), seq(s_len)],
        scratch_shapes=[pltpu.VMEM((rows, w2), BF16), pltpu.VMEM((rows, w2), BF16),
                        pltpu.VMEM((rows, LANES), F32), pltpu.VMEM((rows, LANES), F32), pltpu.VMEM((rows, w2), F32),
                        pltpu.VMEM((rows, LANES), F32), pltpu.VMEM((rows, LANES), F32), pltpu.VMEM((rows, w2), F32)],
    )
    return pl.pallas_call(
        kern,
        grid_spec=grid_spec,
        out_shape=[jax.ShapeDtypeStruct((nb, s_len, w2), F32), jax.ShapeDtypeStruct((nb, s_len, w2), F32)],
        compiler_params=_cparams(("arbitrary", "arbitrary")),
        name="attn_sample",
    )(page_table, qa, qb, kan, van, kbn, vbn, bias, cka, cva, ckb, cvb, *lams, g_subln)


def _resid_kernel(a_ref, w_ref, x_ref, g_ref, o_ref, *acc_scr, nk, mr):
    def finish(acc):
        tm = acc.shape[0]
        if mr == 1:
            o_ref[...] = x_ref[...] + g_ref[...] * acc
        else:
            for s in range(tm // mr):
                o_ref[s * mr:(s + 1) * mr, :] = x_ref[s * mr:(s + 1) * mr, :] + g_ref[...] * acc[s * mr:(s + 1) * mr, :]

    part = jnp.dot(a_ref[...], w_ref[...], preferred_element_type=F32)
    if nk == 1:
        finish(part)
    else:
        k = pl.program_id(2)

        @pl.when(k == 0)
        def _():
            acc_scr[0][...] = part

        @pl.when((k > 0) & (k < nk - 1))
        def _():
            acc_scr[0][...] += part

        @pl.when(k == nk - 1)
        def _():
            finish(acc_scr[0][...] + part)


def _resid_mm(a, w, x, mod, gate_blk, tm, tn, tk):
    rows, kd = a.shape
    n = w.shape[1]
    mr = mod.shape[0]
    nk = kd // tk
    kern = functools.partial(_resid_kernel, nk=nk, mr=mr)
    gate_off = gate_blk * (n // tn)
    return pl.pallas_call(
        kern,
        grid=(rows // tm, n // tn, nk),
        in_specs=[pl.BlockSpec((tm, tk), lambda i, j, k: (i, k)),
                  pl.BlockSpec((tk, tn), lambda i, j, k: (k, j)),
                  pl.BlockSpec((tm, tn), lambda i, j, k: (i, j)),
                  pl.BlockSpec((mr, tn), lambda i, j, k: (0, gate_off + j))],
        out_specs=pl.BlockSpec((tm, tn), lambda i, j, k: (i, j)),
        out_shape=jax.ShapeDtypeStruct((rows, n), F32),
        scratch_shapes=[pltpu.VMEM((tm, tn), F32)] if nk > 1 else [],
        compiler_params=_cparams(("arbitrary", "arbitrary", "arbitrary")),
        name="resid_mm",
    )(a, w, x, mod)


def _up_kernel(h_ref, wg_ref, wv_ref, prev_ref, wc_ref, bc_ref, u_ref, tail_ref, carry_scr, *, hp, shift):
    i = pl.program_id(1)

    @pl.when(i == 0)
    def _():
        carry_scr[...] = prev_ref[...]

    h = h_ref[...]
    g = jnp.dot(h, wg_ref[...], preferred_element_type=F32)
    v = jnp.dot(h, wv_ref[...], preferred_element_type=F32)
    tm = g.shape[0]
    gcat = jnp.concatenate([carry_scr[...], g], axis=0)
    wc = wc_ref[...]
    conv = (bc_ref[...]
            + wc[0:1, :] * gcat[hp - 2 * shift:hp - 2 * shift + tm, :]
            + wc[1:2, :] * gcat[hp - shift:hp - shift + tm, :]
            + wc[2:3, :] * g)
    u_ref[...] = (conv * jax.nn.sigmoid(conv) * v).astype(u_ref.dtype)
    tail = g[tm - hp:, :]
    carry_scr[...] = tail
    tail_ref[...] = tail


def _up(h2, w_up16, prev, w_conv, b_conv, tm, hp, shift):
    rows, d = h2.shape
    ff = w_conv.shape[1]
    tn = _pick(ff, (256, 128))
    nj = ff // tn
    kern = functools.partial(_up_kernel, hp=hp, shift=shift)
    return pl.pallas_call(
        kern,
        grid=(nj, rows // tm),
        in_specs=[pl.BlockSpec((tm, d), lambda j, i: (i, 0)),
                  pl.BlockSpec((d, tn), lambda j, i: (0, j)),
                  pl.BlockSpec((d, tn), lambda j, i: (0, nj + j)),
                  pl.BlockSpec((hp, tn), lambda j, i: (0, j)),
                  pl.BlockSpec((CONV_W, tn), lambda j, i: (0, j)),
                  pl.BlockSpec((1, tn), lambda j, i: (0, j))],
        out_specs=[pl.BlockSpec((tm, tn), lambda j, i: (i, j)),
                   pl.BlockSpec((hp, tn), lambda j, i: (0, j))],
        out_shape=[jax.ShapeDtypeStruct((rows, ff), BF16), jax.ShapeDtypeStruct((hp, ff), F32)],
        scratch_shapes=[pltpu.VMEM((hp, tn), F32)],
        compiler_params=_cparams(("arbitrary", "arbitrary")),
        name="ffn_up",
    )(h2, w_up16, w_up16, prev, w_conv, b_conv.reshape(1, ff))


def _rope_tables(pos):
    inv = ROPE_THETA ** (-jnp.arange(0, HEAD_DIM, 2, dtype=F32) / HEAD_DIM)
    ang = pos.astype(F32)[:, None] * inv[None, :]
    cos, sin = jnp.cos(ang), jnp.sin(ang)
    return jnp.concatenate([cos, cos], axis=1), jnp.concatenate([-sin, sin], axis=1)


def _layout(d, in_width):
    w2 = d // 2
    n_idx = (in_width - 6 * w2 - IDX_DIM) // (IDX_DIM + 1)
    qi = n_idx * IDX_DIM
    tn = min(512, w2)
    assert qi % w2 == 0 and w2 % tn == 0 and n_idx <= LANES
    lay = dict(w2=w2, n_idx=n_idx, tn=tn, iq=0, qa=qi, ka=qi + w2, va=qi + 2 * w2, qb=qi + 3 * w2,
               kb=qi + 4 * w2, vb=qi + 5 * w2, sp=qi + 6 * w2, total=qi + 6 * w2 + tn)
    return lay


def _permute_w_in(w, lay):
    d = w.shape[0]
    w2, qi, n_idx = lay["w2"], lay["n_idx"] * IDX_DIM, lay["n_idx"]
    o = 0
    seg = {}
    for name, size in (("qa", w2), ("ka", w2), ("va", w2), ("iq", qi), ("ik", IDX_DIM), ("iw", n_idx),
                       ("qb", w2), ("kb", w2), ("vb", w2)):
        seg[name] = w[:, o:o + size]
        o += size
    pad = jnp.zeros((d, lay["tn"] - IDX_DIM - n_idx), w.dtype)
    cols = [seg[k] for k in ("iq", "qa", "ka", "va", "qb", "kb", "vb", "ik", "iw")] + [pad]
    return jnp.concatenate(cols, axis=1).astype(BF16)


def _layer(x2d, mod, pos, prev, attend, lw, lay, tm_rows, tm_norm, hp, shift):
    g_attn, w_perm, w_o16, g_ffn, w_up16, w_conv, b_conv, w_down16, g_final = lw
    rows, d = x2d.shape
    cosf, sinf = _rope_tables(pos)
    h = _norm_mod(x2d, g_attn, mod, 1, 0, tm_norm)
    p32, p16 = _proj(h, w_perm, cosf, sinf, lay, tm_rows)
    o16 = attend(p32, p16)
    tn = _pick(d, (512, 256, 128))
    x1 = _resid_mm(o16, w_o16, x2d, mod, 2, tm_rows, tn, d)
    h2 = _norm_mod(x1, g_ffn, mod, 4, 3, tm_norm)
    u, tail = _up(h2, w_up16, prev, w_conv, b_conv, tm_rows, hp, shift)
    ff = u.shape[1]
    tk = ff // 2 if (ff // 2) % LANES == 0 else ff
    x2 = _resid_mm(u, w_down16, x1, mod, 5, min(tm_rows, 512), tn, tk)
    y = _final_norm(x2, g_final, tm_norm)
    return y, p32, tail


def kernel(x_prompt, x_sample, cache_k_a, cache_v_a, cache_k_idx, cache_k_b, cache_v_b, state_conv, page_table,
           c_prompt, c_sample, w_ada, b_ada, g_attn, w_in, lam_q1, lam_k1, lam_q2, lam_k2, g_subln, w_o, g_ffn,
           w_up, w_conv, b_conv, w_down, g_final):
    depth = w_ada.shape[0]
    assert depth == 1 and x_prompt.shape[0] == 1
    bsz, t_len, d = x_prompt.shape
    nb, s_len, _ = x_sample.shape
    n_pages = page_table.shape[1]
    past_len = n_pages * PAGE_SIZE
    ff = w_conv.shape[-1]
    lay = _layout(d, w_in.shape[-1])
    w2, n_idx = lay["w2"], lay["n_idx"]
    nha, nhb = w2 // HEAD_DIM, w2 // (2 * HEAD_DIM)
    l = 0
    lam_init = 0.8 - 0.6 * math.exp(-0.3 * l)

    w_perm = _permute_w_in(w_in[l], lay)
    lw = (g_attn[l], w_perm, w_o[l].astype(BF16), g_ffn[l], w_up[l].astype(BF16), w_conv[l], b_conv[l],
          w_down[l].astype(BF16), g_final)
    lams = tuple(a[l].reshape(1, HEAD_DIM) for a in (lam_q1, lam_k1, lam_q2, lam_k2))
    gs = g_subln[l].reshape(1, 2 * HEAD_DIM)

    n_c = nb + 16
    c_all = jnp.concatenate([c_sample, c_prompt, jnp.zeros((n_c - nb - 1, d), F32)], axis=0).astype(BF16)
    mod = _ada(c_all, w_ada[l], b_ada[l])
    mod_s, mod_p = mod[:nb], mod[nb:nb + 1]

    tq = _pick(t_len, (256, 128))
    tk = _pick(t_len, (512, 256, 128))
    k_sel_p = min(TOPK_MAX, t_len // 4)

    def attend_p(p32, p16):
        bias = _score_p(p16, p32, lay, tq, tk, k_sel_p)
        oa = _sattn_p(p16, bias, lay, tq, tk)
        ob = _dattn_p(p16, lams, gs, lay, tq, tk, lam_init)
        return jnp.concatenate([oa, ob], axis=1)

    tm_p = _pick(t_len, (1024, 512, 256, 128))
    y_p, p32_p, tail_p = _layer(x_prompt[0], mod_p, jnp.arange(t_len), jnp.zeros((8, ff), F32), attend_p, lw, lay,
                                tm_p, _pick(t_len, (512, 256, 128)), 8, 1)

    def seg(p32, name, width):
        return p32[:, lay[name]:lay[name] + width]

    outs_p = (
        y_p.reshape(1, t_len, d),
        seg(p32_p, "ka", w2).reshape(1, 1, t_len, nha, HEAD_DIM),
        seg(p32_p, "va", w2).reshape(1, 1, t_len, nha, HEAD_DIM),
        seg(p32_p, "sp", IDX_DIM).reshape(1, 1, t_len, IDX_DIM),
        seg(p32_p, "kb", w2).reshape(1, 1, t_len, nhb, 2, HEAD_DIM),
        seg(p32_p, "vb", w2).reshape(1, 1, t_len, nhb, 2 * HEAD_DIM),
        tail_p[8 - (CONV_W - 1):].reshape(1, 1, CONV_W - 1, ff),
    )

    rows_s = s_len * nb
    xs_tm = jnp.transpose(x_sample, (1, 0, 2)).reshape(rows_s, d)
    pos_s = past_len + jnp.repeat(jnp.arange(s_len), nb)
    prev_s = jnp.transpose(state_conv[l], (1, 0, 2)).reshape((CONV_W - 1) * nb, ff)
    k_sel_s = min(TOPK_MAX, (past_len + s_len) // 4)
    n_phys = cache_k_a.shape[1]

    def to_bm(a):
        return jnp.transpose(a.reshape(s_len, nb, a.shape[1]), (1, 0, 2))

    def pad8(a):
        return jnp.pad(a, ((0, 0), (0, 8 - s_len), (0, 0)))

    bm_cache = {}

    def attend_s(p32, p16):
        bm = to_bm(p32)
        bm_cache["bm"] = bm
        sl = lambda name, width: bm[:, :, lay[name]:lay[name] + width]
        iq = sl("iq", n_idx * IDX_DIM).reshape(nb, s_len * n_idx, IDX_DIM)
        iw = bm[:, :, lay["sp"] + IDX_DIM:lay["sp"] + IDX_DIM + n_idx].reshape(nb, s_len * n_idx, 1)
        scores = _score_s(page_table, iq, iw, pad8(sl("sp", IDX_DIM)),
                          cache_k_idx[l].reshape(n_phys, PAGE_SIZE, IDX_DIM), past_len, s_len)
        l_pad = scores.shape[2]
        bias = _thresh(scores.reshape(nb * s_len, l_pad), k_sel_s).reshape(nb, s_len, l_pad)
        oa, ob = _attn_s(page_table, sl("qa", w2), sl("qb", w2), pad8(sl("ka", w2)), pad8(sl("va", w2)),
                         pad8(sl("kb", w2)), pad8(sl("vb", w2)), bias,
                         cache_k_a[l].reshape(n_phys, PAGE_SIZE, w2), cache_v_a[l].reshape(n_phys, PAGE_SIZE, w2),
                         cache_k_b[l].reshape(n_phys, PAGE_SIZE, w2), cache_v_b[l].reshape(n_phys, PAGE_SIZE, w2),
                         lams, gs, lam_init)
        o = jnp.concatenate([oa, ob], axis=2)
        return jnp.transpose(o, (1, 0, 2)).reshape(rows_s, d).astype(BF16)

    y_s, _, tail_s = _layer(xs_tm, mod_s, pos_s, prev_s, attend_s, lw, lay, rows_s, nb, (CONV_W - 1) * nb, nb)
    bm = bm_cache["bm"]
    sb = lambda name, width: bm[:, :, lay[name]:lay[name] + width]
    outs_s = (
        jnp.transpose(y_s.reshape(s_len, nb, d), (1, 0, 2)),
        sb("ka", w2).reshape(1, nb, s_len, nha, HEAD_DIM),
        sb("va", w2).reshape(1, nb, s_len, nha, HEAD_DIM),
        sb("sp", IDX_DIM).reshape(1, nb, s_len, IDX_DIM),
        sb("kb", w2).reshape(1, nb, s_len, nhb, 2, HEAD_DIM),
        sb("vb", w2).reshape(1, nb, s_len, nhb, 2 * HEAD_DIM),
        jnp.transpose(tail_s.reshape(CONV_W - 1, nb, ff), (1, 0, 2)).reshape(1, nb, CONV_W - 1, ff),
    )
    return (outs_p[0], outs_s[0]) + outs_p[1:] + outs_s[1:]
```

```python
import functools
import math

import jax
import jax.numpy as jnp
from jax import lax
from jax.experimental import pallas as pl
from jax.experimental.pallas import tpu as pltpu

F32 = jnp.float32
BF16 = jnp.bfloat16

HEAD_DIM = 128
IDX_DIM = 128
PAGE_SIZE = 128
TOPK_MAX = 256
CONV_W = 3
ROPE_THETA = 10000.0
EPS = 1e-6
HEAD_SCALE = HEAD_DIM ** -0.5
SCORE_SCALE = HEAD_SCALE * math.log2(math.e)
IDX_SCALE = IDX_DIM ** -0.5
LANES = 128
NEG = -1e30
INT_MIN = -(2 ** 31)
KEY_NEG_INF = INT_MIN + 0x7FFFFF
VMEM_LIMIT = 56 * 1024 * 1024


def _cparams(sem):
    return pltpu.CompilerParams(dimension_semantics=sem, vmem_limit_bytes=VMEM_LIMIT)


def _pick(n, cands):
    for c in cands:
        if n % c == 0:
            return c
    return n


def _ada_kernel(c_ref, w_ref, b_ref, o_ref):
    acc = jnp.dot(c_ref[...], w_ref[...].astype(BF16), preferred_element_type=F32)
    o_ref[...] = acc + b_ref[...]


def _ada(c_all, w_ada, b_ada):
    r, d = c_all.shape
    n = w_ada.shape[1]
    tn = _pick(n, (512, 256, 128))
    return pl.pallas_call(
        _ada_kernel,
        grid=(n // tn,),
        in_specs=[pl.BlockSpec((r, d), lambda j: (0, 0)),
                  pl.BlockSpec((d, tn), lambda j: (0, j)),
                  pl.BlockSpec((1, tn), lambda j: (0, j))],
        out_specs=pl.BlockSpec((r, tn), lambda j: (0, j)),
        out_shape=jax.ShapeDtypeStruct((r, n), F32),
        compiler_params=_cparams(("arbitrary",)),
        name="ada",
    )(c_all, w_ada, b_ada.reshape(1, n))


def _norm_mod_kernel(x_ref, g_ref, sc_ref, sh_ref, o_ref):
    x = x_ref[...]
    y = x * lax.rsqrt(jnp.mean(x * x, axis=-1, keepdims=True) + EPS) * g_ref[...]
    o_ref[...] = (y * (1.0 + sc_ref[...]) + sh_ref[...]).astype(o_ref.dtype)


def _norm_mod(x, g, mod, sc_blk, sh_blk, tm):
    rows, d = x.shape
    mr = mod.shape[0]
    assert mr == 1 or mr == tm
    return pl.pallas_call(
        _norm_mod_kernel,
        grid=(rows // tm,),
        in_specs=[pl.BlockSpec((tm, d), lambda i: (i, 0)),
                  pl.BlockSpec((1, d), lambda i: (0, 0)),
                  pl.BlockSpec((mr, d), lambda i: (0, sc_blk)),
                  pl.BlockSpec((mr, d), lambda i: (0, sh_blk))],
        out_specs=pl.BlockSpec((tm, d), lambda i: (i, 0)),
        out_shape=jax.ShapeDtypeStruct((rows, d), BF16),
        compiler_params=_cparams(("arbitrary",)),
        name="norm_mod",
    )(x, g.reshape(1, d), mod, mod)


def _final_norm_kernel(x_ref, g_ref, o_ref):
    x = x_ref[...]
    o_ref[...] = x * lax.rsqrt(jnp.mean(x * x, axis=-1, keepdims=True) + EPS) * g_ref[...]


def _final_norm(x, g, tm):
    rows, d = x.shape
    return pl.pallas_call(
        _final_norm_kernel,
        grid=(rows // tm,),
        in_specs=[pl.BlockSpec((tm, d), lambda i: (i, 0)),
                  pl.BlockSpec((1, d), lambda i: (0, 0))],
        out_specs=pl.BlockSpec((tm, d), lambda i: (i, 0)),
        out_shape=jax.ShapeDtypeStruct((rows, d), F32),
        compiler_params=_cparams(("arbitrary",)),
        name="final_norm",
    )(x, g.reshape(1, d))


def _rope_cols(a, cos, sin):
    outs = []
    for c in range(a.shape[1] // HEAD_DIM):
        xh = a[:, c * HEAD_DIM:(c + 1) * HEAD_DIM]
        outs.append(xh * cos + pltpu.roll(xh, HEAD_DIM // 2, 1) * sin)
    return outs[0] if len(outs) == 1 else jnp.concatenate(outs, axis=1)


def _proj_kernel(h_ref, w_ref, cos_ref, sin_ref, o32_ref, o16_ref, *, plain_lo, plain_hi, special_j, iw_scale):
    j = pl.program_id(1)
    acc = jnp.dot(h_ref[...], w_ref[...], preferred_element_type=F32)
    is_plain = ((j >= plain_lo[0]) & (j < plain_hi[0])) | ((j >= plain_lo[1]) & (j < plain_hi[1]))
    is_special = j == special_j

    def emit(v):
        o32_ref[...] = v
        o16_ref[...] = v.astype(BF16)

    @pl.when(is_plain)
    def _():
        emit(acc)

    @pl.when(is_special)
    def _():
        parts = [_rope_cols(acc[:, :IDX_DIM], cos_ref[...], sin_ref[...]), acc[:, IDX_DIM:2 * IDX_DIM] * iw_scale]
        if acc.shape[1] > 2 * IDX_DIM:
            parts.append(acc[:, 2 * IDX_DIM:])
        emit(jnp.concatenate(parts, axis=1))

    @pl.when(jnp.logical_not(is_plain | is_special))
    def _():
        emit(_rope_cols(acc, cos_ref[...], sin_ref[...]))


def _proj(h, w_perm, cosf, sinf, lay, tm):
    rows, d = h.shape
    n = w_perm.shape[1]
    tn = lay["tn"]
    kern = functools.partial(
        _proj_kernel,
        plain_lo=(lay["va"] // tn, lay["vb"] // tn), plain_hi=(lay["qb"] // tn, lay["sp"] // tn),
        special_j=lay["sp"] // tn, iw_scale=lay["n_idx"] ** -0.5)
    return pl.pallas_call(
        kern,
        grid=(rows // tm, n // tn),
        in_specs=[pl.BlockSpec((tm, d), lambda i, j: (i, 0)),
                  pl.BlockSpec((d, tn), lambda i, j: (0, j)),
                  pl.BlockSpec((tm, HEAD_DIM), lambda i, j: (i, 0)),
                  pl.BlockSpec((tm, HEAD_DIM), lambda i, j: (i, 0))],
        out_specs=[pl.BlockSpec((tm, tn), lambda i, j: (i, j)),
                   pl.BlockSpec((tm, tn), lambda i, j: (i, j))],
        out_shape=[jax.ShapeDtypeStruct((rows, n), F32), jax.ShapeDtypeStruct((rows, n), BF16)],
        compiler_params=_cparams(("arbitrary", "arbitrary")),
        name="proj",
    )(h, w_perm, cosf, sinf)


def _float_key(x):
    bits = pltpu.bitcast(x, jnp.int32)
    return bits ^ ((bits >> 31) & 0x7FFFFFFF)


def _topk_bias(key_ref, thr_ref, o_ref, *, rows, nch, total_ch, k, rg, unroll):
    for g in range(rows // rg):
        r0 = g * rg

        def bit_body(b, t, r0=r0):
            cand = t + lax.shift_left(jnp.int32(1), jnp.int32(31) - b)

            def ch_body(c, cnt):
                for u in range(unroll):
                    sl = pl.ds(pl.multiple_of((c * unroll + u) * LANES, LANES), LANES)
                    cnt = cnt + jnp.where(key_ref[r0:r0 + rg, sl] >= cand, 1.0, 0.0)
                return cnt

            cnt = lax.fori_loop(0, nch // unroll, ch_body, jnp.zeros((rg, LANES), F32))
            tot = jnp.sum(cnt, axis=1, keepdims=True)
            return jnp.where(tot >= float(k), cand, t)

        t = lax.fori_loop(0, 32, bit_body, jnp.full((rg, LANES), INT_MIN, jnp.int32))
        thr_ref[r0:r0 + rg, :] = jnp.maximum(t, KEY_NEG_INF + 1)

    thr = thr_ref[...]

    def out_body(c, carry):
        sl = pl.ds(pl.multiple_of(c * LANES, LANES), LANES)
        o_ref[:, sl] = jnp.where(key_ref[:, sl] >= thr, 0.0, NEG).astype(o_ref.dtype)
        return carry

    def fill_body(c, carry):
        sl = pl.ds(pl.multiple_of(c * LANES, LANES), LANES)
        o_ref[:, sl] = jnp.full((rows, LANES), NEG, o_ref.dtype)
        return carry

    lax.fori_loop(0, nch, out_body, 0)
    lax.fori_loop(nch, total_ch, fill_body, 0)


def _score_p_kernel(iq_ref, ik_ref, iw_ref, o_ref, key_scr, wb_scr, thr_scr, *, tq, tk, n_idx, k_sel, nj, t_len):
    i = pl.program_id(0)
    j = pl.program_id(1)
    jmax = ((i + 1) * tq - 1) // tk

    @pl.when(j == 0)
    def _():
        w = iw_ref[...] * IDX_SCALE
        for h in range(n_idx):
            wb_scr[h] = jnp.broadcast_to(w[:, h:h + 1], (tq, LANES))

    @pl.when(j <= jmax)
    def _():
        kb = ik_ref[...]
        acc = jnp.zeros((tq, tk), F32)
        for h in range(n_idx):
            s = lax.dot_general(iq_ref[:, h * IDX_DIM:(h + 1) * IDX_DIM], kb, (((1,), (1,)), ((), ())),
                                preferred_element_type=F32)
            acc = acc + jnp.maximum(s, 0.0) * pltpu.repeat(wb_scr[h], tk // LANES, axis=1)
        rows = i * tq + lax.broadcasted_iota(jnp.int32, (tq, tk), 0)
        cols = j * tk + lax.broadcasted_iota(jnp.int32, (tq, tk), 1)
        acc = jnp.where(cols <= rows, acc, -jnp.inf)
        key_scr[:, pl.ds(pl.multiple_of(j * tk, tk), tk)] = _float_key(acc)

    @pl.when(j == nj - 1)
    def _():
        _topk_bias(key_scr, thr_scr, o_ref, rows=tq, nch=(jmax + 1) * (tk // LANES), total_ch=t_len // LANES,
                   k=k_sel, rg=min(tq, 64), unroll=tk // LANES)


def _score_p(p16, p32, lay, tq, tk, k_sel):
    t_len = p16.shape[0]
    n_idx = lay["n_idx"]
    qi = n_idx * IDX_DIM
    ni, nj = t_len // tq, t_len // tk
    sp_blk = lay["sp"] // IDX_DIM
    kern = functools.partial(_score_p_kernel, tq=tq, tk=tk, n_idx=n_idx, k_sel=k_sel, nj=nj, t_len=t_len)
    return pl.pallas_call(
        kern,
        grid=(ni, nj),
        in_specs=[pl.BlockSpec((tq, qi), lambda i, j: (i, 0)),
                  pl.BlockSpec((tk, IDX_DIM), lambda i, j: (jnp.minimum(j, ((i + 1) * tq - 1) // tk), sp_blk)),
                  pl.BlockSpec((tq, IDX_DIM), lambda i, j: (i, sp_blk + 1))],
        out_specs=pl.BlockSpec((tq, t_len), lambda i, j: (i, 0)),
        out_shape=jax.ShapeDtypeStruct((t_len, t_len), BF16),
        scratch_shapes=[pltpu.VMEM((tq, t_len), jnp.int32),
                        pltpu.VMEM((n_idx, tq, LANES), F32),
                        pltpu.VMEM((tq, LANES), jnp.int32)],
        compiler_params=_cparams(("arbitrary", "arbitrary")),
        name="score_prompt",
    )(p16, p16, p32)


def _thresh_kernel(s_ref, o_ref, key_scr, thr_scr, *, rows, nch, k_sel):
    key_scr[...] = _float_key(s_ref[...])
    _topk_bias(key_scr, thr_scr, o_ref, rows=rows, nch=nch, total_ch=nch, k=k_sel, rg=min(rows, 64), unroll=nch)


def _thresh(s, k_sel):
    rows, l = s.shape
    kern = functools.partial(_thresh_kernel, rows=rows, nch=l // LANES, k_sel=k_sel)
    return pl.pallas_call(
        kern,
        grid=(1,),
        in_specs=[pl.BlockSpec((rows, l), lambda i: (0, 0))],
        out_specs=pl.BlockSpec((rows, l), lambda i: (0, 0)),
        out_shape=jax.ShapeDtypeStruct((rows, l), F32),
        scratch_shapes=[pltpu.VMEM((rows, l), jnp.int32), pltpu.VMEM((rows, LANES), jnp.int32)],
        compiler_params=_cparams(("arbitrary",)),
        name="thresh_sample",
    )(s)


def _online_step(s, v, m_scr, l_scr, hd, acc_ref, c0, c1):
    tk = s.shape[1]
    m_prev = m_scr[hd]
    l_prev = l_scr[hd]
    m_new = jnp.maximum(m_prev, jnp.max(s, axis=1, keepdims=True))
    alpha = jnp.exp2(m_prev - m_new)
    p = jnp.exp2(s - pltpu.repeat(m_new, tk // LANES, axis=1))
    l_scr[hd] = alpha * l_prev + jnp.sum(p, axis=1, keepdims=True)
    m_scr[hd] = m_new
    pv = jnp.dot(p.astype(BF16), v, preferred_element_type=F32)
    arep = alpha if (c1 - c0) == LANES else pltpu.repeat(alpha, (c1 - c0) // LANES, axis=1)
    acc_ref[:, c0:c1] = acc_ref[:, c0:c1] * arep + pv


def _qk(q, k):
    return lax.dot_general(q, k, (((1,), (1,)), ((), ())), preferred_element_type=F32)


def _sattn_p_kernel(q_ref, k_ref, v_ref, b_ref, o_ref, m_scr, l_scr, acc_scr, *, tq, tk, nh, nj):
    i = pl.program_id(0)
    j = pl.program_id(1)
    jmax = ((i + 1) * tq - 1) // tk

    @pl.when(j == 0)
    def _():
        m_scr[...] = jnp.full(m_scr.shape, NEG, F32)
        l_scr[...] = jnp.zeros(l_scr.shape, F32)
        acc_scr[...] = jnp.zeros(acc_scr.shape, F32)

    @pl.when(j <= jmax)
    def _():
        bias = b_ref[...].astype(F32)
        for h in range(nh):
            c0, c1 = h * HEAD_DIM, (h + 1) * HEAD_DIM
            s = _qk(q_ref[:, c0:c1], k_ref[:, c0:c1]) * SCORE_SCALE + bias
            _online_step(s, v_ref[:, c0:c1], m_scr, l_scr, h, acc_scr, c0, c1)

    @pl.when(j == nj - 1)
    def _():
        for h in range(nh):
            c0, c1 = h * HEAD_DIM, (h + 1) * HEAD_DIM
            o_ref[:, c0:c1] = (acc_scr[:, c0:c1] / l_scr[h]).astype(o_ref.dtype)


def _sattn_p(p16, bias, lay, tq, tk):
    t_len = p16.shape[0]
    w2 = lay["w2"]
    nh = w2 // HEAD_DIM
    ni, nj = t_len // tq, t_len // tk
    jm = lambda i, j: jnp.minimum(j, ((i + 1) * tq - 1) // tk)
    kern = functools.partial(_sattn_p_kernel, tq=tq, tk=tk, nh=nh, nj=nj)
    return pl.pallas_call(
        kern,
        grid=(ni, nj),
        in_specs=[pl.BlockSpec((tq, w2), lambda i, j: (i, lay["qa"] // w2)),
                  pl.BlockSpec((tk, w2), lambda i, j: (jm(i, j), lay["ka"] // w2)),
                  pl.BlockSpec((tk, w2), lambda i, j: (jm(i, j), lay["va"] // w2)),
                  pl.BlockSpec((tq, tk), lambda i, j: (i, jm(i, j)))],
        out_specs=pl.BlockSpec((tq, w2), lambda i, j: (i, 0)),
        out_shape=jax.ShapeDtypeStruct((t_len, w2), BF16),
        scratch_shapes=[pltpu.VMEM((nh, tq, LANES), F32), pltpu.VMEM((nh, tq, LANES), F32),
                        pltpu.VMEM((tq, w2), F32)],
        compiler_params=_cparams(("arbitrary", "arbitrary")),
        name="sparse_attn_prompt",
    )(p16, p16, p16, bias)


def _lam_value(lq1, lk1, lq2, lk2, lam_init):
    a = jnp.sum(lq1[...] * lk1[...], axis=1, keepdims=True)
    b = jnp.sum(lq2[...] * lk2[...], axis=1, keepdims=True)
    return jnp.exp(a) - jnp.exp(b) + lam_init


def _subln(o, g, lam_init):
    return o * lax.rsqrt(jnp.mean(o * o, axis=1, keepdims=True) + EPS) * g * (1.0 - lam_init)


def _dattn_p_kernel(q_ref, k_ref, v_ref, lq1, lk1, lq2, lk2, g_ref, o_ref, m_scr, l_scr, acc0_scr, acc1_scr,
                    *, tq, tk, nhb, nj, lam_init):
    i = pl.program_id(0)
    j = pl.program_id(1)
    jmax = ((i + 1) * tq - 1) // tk
    dv = 2 * HEAD_DIM

    @pl.when(j == 0)
    def _():
        m_scr[...] = jnp.full(m_scr.shape, NEG, F32)
        l_scr[...] = jnp.zeros(l_scr.shape, F32)
        acc0_scr[...] = jnp.zeros(acc0_scr.shape, F32)
        acc1_scr[...] = jnp.zeros(acc1_scr.shape, F32)

    @pl.when(j <= jmax)
    def _():
        rows = i * tq + lax.broadcasted_iota(jnp.int32, (tq, tk), 0)
        cols = j * tk + lax.broadcasted_iota(jnp.int32, (tq, tk), 1)
        bias = jnp.where(cols <= rows, 0.0, NEG)
        for h in range(nhb):
            v = v_ref[:, h * dv:(h + 1) * dv]
            for c, acc in ((0, acc0_scr), (1, acc1_scr)):
                hd = 2 * h + c
                c0, c1 = hd * HEAD_DIM, (hd + 1) * HEAD_DIM
                s = _qk(q_ref[:, c0:c1], k_ref[:, c0:c1]) * SCORE_SCALE + bias
                _online_step(s, v, m_scr, l_scr, hd, acc, h * dv, (h + 1) * dv)

    @pl.when(j == nj - 1)
    def _():
        lam = _lam_value(lq1, lk1, lq2, lk2, lam_init)
        g = g_ref[...]
        for h in range(nhb):
            c0, c1 = h * dv, (h + 1) * dv
            o0 = acc0_scr[:, c0:c1] / pltpu.repeat(l_scr[2 * h], 2, axis=1)
            o1 = acc1_scr[:, c0:c1] / pltpu.repeat(l_scr[2 * h + 1], 2, axis=1)
            o_ref[:, c0:c1] = _subln(o0 - lam * o1, g, lam_init).astype(o_ref.dtype)


def _dattn_p(p16, lams, g_subln, lay, tq, tk, lam_init):
    t_len = p16.shape[0]
    w2 = lay["w2"]
    nhb = w2 // (2 * HEAD_DIM)
    ni, nj = t_len // tq, t_len // tk
    jm = lambda i, j: jnp.minimum(j, ((i + 1) * tq - 1) // tk)
    kern = functools.partial(_dattn_p_kernel, tq=tq, tk=tk, nhb=nhb, nj=nj, lam_init=lam_init)
    vec = pl.BlockSpec((1, HEAD_DIM), lambda i, j: (0, 0))
    return pl.pallas_call(
        kern,
        grid=(ni, nj),
        in_specs=[pl.BlockSpec((tq, w2), lambda i, j: (i, lay["qb"] // w2)),
                  pl.BlockSpec((tk, w2), lambda i, j: (jm(i, j), lay["kb"] // w2)),
                  pl.BlockSpec((tk, w2), lambda i, j: (jm(i, j), lay["vb"] // w2)),
                  vec, vec, vec, vec,
                  pl.BlockSpec((1, 2 * HEAD_DIM), lambda i, j: (0, 0))],
        out_specs=pl.BlockSpec((tq, w2), lambda i, j: (i, 0)),
        out_shape=jax.ShapeDtypeStruct((t_len, w2), BF16),
        scratch_shapes=[pltpu.VMEM((2 * nhb, tq, LANES), F32), pltpu.VMEM((2 * nhb, tq, LANES), F32),
                        pltpu.VMEM((tq, w2), F32), pltpu.VMEM((tq, w2), F32)],
        compiler_params=_cparams(("arbitrary", "arbitrary")),
        name="diff_attn_prompt",
    )(p16, p16, p16, *lams, g_subln)


def _score_s_kernel(pt_ref, iq_ref, iw_ref, iknew_ref, *rest, n_pages, s_len, n_idx, past_len):
    page_refs = rest[:n_pages]
    o_ref = rest[n_pages]
    q = iq_ref[...].astype(BF16)
    w = iw_ref[...] * IDX_SCALE
    knew = jnp.concatenate([iknew_ref[...], jnp.zeros((PAGE_SIZE - 8, IDX_DIM), F32)], axis=0)
    blocks = [r[...] for r in page_refs] + [knew]
    for p, kb in enumerate(blocks):
        s = _qk(q, kb.astype(BF16))
        s = jnp.maximum(s, 0.0) * w
        sc = jnp.sum(s.reshape(s_len, n_idx, PAGE_SIZE), axis=1)
        cols = p * PAGE_SIZE + lax.broadcasted_iota(jnp.int32, (s_len, PAGE_SIZE), 1)
        qpos = past_len + lax.broadcasted_iota(jnp.int32, (s_len, PAGE_SIZE), 0)
        o_ref[:, p * PAGE_SIZE:(p + 1) * PAGE_SIZE] = jnp.where(cols <= qpos, sc, -jnp.inf)


def _score_s(page_table, iq_s, iw_s, iknew8, ck_idx, past_len, s_len):
    nb, rows, _ = iq_s.shape
    n_pages = page_table.shape[1]
    n_idx = rows // s_len
    l_pad = (n_pages + 1) * PAGE_SIZE
    kern = functools.partial(_score_s_kernel, n_pages=n_pages, s_len=s_len, n_idx=n_idx, past_len=past_len)
    page_specs = [pl.BlockSpec((None, PAGE_SIZE, IDX_DIM), functools.partial(lambda b, pt, p: (pt[b, p], 0, 0), p=p))
                  for p in range(n_pages)]
    grid_spec = pltpu.PrefetchScalarGridSpec(
        num_scalar_prefetch=1,
        grid=(nb,),
        in_specs=[pl.BlockSpec((None, rows, IDX_DIM), lambda b, pt: (b, 0, 0)),
                  pl.BlockSpec((None, rows, 1), lambda b, pt: (b, 0, 0)),
                  pl.BlockSpec((None, 8, IDX_DIM), lambda b, pt: (b, 0, 0))] + page_specs,
        out_specs=pl.BlockSpec((None, s_len, l_pad), lambda b, pt: (b, 0, 0)),
    )
    return pl.pallas_call(
        kern,
        grid_spec=grid_spec,
        out_shape=jax.ShapeDtypeStruct((nb, s_len, l_pad), F32),
        compiler_params=_cparams(("arbitrary",)),
        name="score_sample",
    )(page_table, iq_s, iw_s, iknew8, *([ck_idx] * n_pages))


def _blockdiag_q(q, n_slots, width):
    s_len, w = q.shape
    rep = jnp.broadcast_to(q[:, None, :], (s_len, n_slots, w)).reshape(s_len * n_slots, w)
    return jnp.where(_diag_mask(s_len, n_slots, w, width), rep, 0.0)


def _diag_mask(s_len, n_slots, w, width):
    r = lax.broadcasted_iota(jnp.int32, (s_len * n_slots, w), 0) % n_slots
    c = lax.broadcasted_iota(jnp.int32, (s_len * n_slots, w), 1) // width
    return r == c


def _page_step(qbd_scr, k, v, bias, m_scr, l_scr, acc_scr):
    s = _qk(qbd_scr[...], k) * SCORE_SCALE + bias
    m_prev = m_scr[...]
    m_new = jnp.maximum(m_prev, jnp.max(s, axis=1, keepdims=True))
    alpha = jnp.exp2(m_prev - m_new)
    p = jnp.exp2(s - m_new)
    l_scr[...] = alpha * l_scr[...] + jnp.sum(p, axis=1, keepdims=True)
    m_scr[...] = m_new
    pv = jnp.dot(p.astype(BF16), v, preferred_element_type=F32)
    acc_scr[...] = acc_scr[...] * pltpu.repeat(alpha, acc_scr.shape[1] // LANES, axis=1) + pv


def _attn_s_kernel(pt_ref, qa_ref, qb_ref, kan_ref, van_ref, kbn_ref, vbn_ref, bias_ref,
                   cka_ref, cva_ref, ckb_ref, cvb_ref, lq1, lk1, lq2, lk2, g_ref,
                   oa_ref, ob_ref,
                   qa_scr, qb_scr, ma_scr, la_scr, acca_scr, mb_scr, lb_scr, accb_scr,
                   *, n_pages, s_len, nslot, w2, lam_init):
    p = pl.program_id(1)
    rows = s_len * nslot

    @pl.when(p == 0)
    def _():
        qa_scr[...] = _blockdiag_q(qa_ref[...], nslot, HEAD_DIM).astype(BF16)
        qb_scr[...] = _blockdiag_q(qb_ref[...], nslot, HEAD_DIM).astype(BF16)
        for m_scr, l_scr, acc_scr in ((ma_scr, la_scr, acca_scr), (mb_scr, lb_scr, accb_scr)):
            m_scr[...] = jnp.full(m_scr.shape, NEG, F32)
            l_scr[...] = jnp.zeros(l_scr.shape, F32)
            acc_scr[...] = jnp.zeros(acc_scr.shape, F32)

    def bias_rows(b4):
        return jnp.broadcast_to(b4[:, None, :], (s_len, nslot, PAGE_SIZE)).reshape(rows, PAGE_SIZE)

    def load_page(ref, starts):
        return jnp.concatenate([ref[pl.ds(s, PAGE_SIZE, stride=nslot), :].astype(BF16) for s in starts], axis=1)

    slots = tuple(range(nslot))
    vb_slots = tuple((s % 2) * (nslot // 2) + s // 2 for s in slots)

    @pl.when(p < n_pages)
    def _():
        _page_step(qa_scr, load_page(cka_ref, slots), load_page(cva_ref, slots), bias_rows(bias_ref[...]),
                   ma_scr, la_scr, acca_scr)
        _page_step(qb_scr, load_page(ckb_ref, slots), load_page(cvb_ref, vb_slots), 0.0,
                   mb_scr, lb_scr, accb_scr)

    @pl.when(p == n_pages)
    def _():
        pad = lambda r: jnp.concatenate([r[...], jnp.zeros((PAGE_SIZE - 8, w2), F32)], axis=0).astype(BF16)
        _page_step(qa_scr, pad(kan_ref), pad(van_ref), bias_rows(bias_ref[...]), ma_scr, la_scr, acca_scr)
        kcol = lax.broadcasted_iota(jnp.int32, (s_len, PAGE_SIZE), 1)
        qrow = lax.broadcasted_iota(jnp.int32, (s_len, PAGE_SIZE), 0)
        causal = jnp.where(kcol <= qrow, 0.0, NEG)
        _page_step(qb_scr, pad(kbn_ref), pad(vbn_ref), bias_rows(causal), mb_scr, lb_scr, accb_scr)

        oa = jnp.where(_diag_mask(s_len, nslot, w2, HEAD_DIM), acca_scr[...] / la_scr[...][:, :1], 0.0)
        oa_ref[...] = jnp.sum(oa.reshape(s_len, nslot, w2), axis=1)

        nb = accb_scr[...] / lb_scr[...][:, :1]
        r = lax.broadcasted_iota(jnp.int32, (rows, w2), 0) % nslot
        c = lax.broadcasted_iota(jnp.int32, (rows, w2), 1) // (2 * HEAD_DIM)
        own = (r // 2) == c
        o0 = jnp.sum(jnp.where(own & (r % 2 == 0), nb, 0.0).reshape(s_len, nslot, w2), axis=1)
        o1 = jnp.sum(jnp.where(own & (r % 2 == 1), nb, 0.0).reshape(s_len, nslot, w2), axis=1)
        ob = o0 - _lam_value(lq1, lk1, lq2, lk2, lam_init) * o1
        g = g_ref[...]
        dv = 2 * HEAD_DIM
        for h in range(w2 // dv):
            ob_ref[:, h * dv:(h + 1) * dv] = _subln(ob[:, h * dv:(h + 1) * dv], g, lam_init)


def _attn_s(page_table, qa, qb, kan, van, kbn, vbn, bias, cka, cva, ckb, cvb, lams, g_subln, lam_init):
    nb, s_len, w2 = qa.shape
    n_pages = page_table.shape[1]
    nslot = w2 // HEAD_DIM
    rows = s_len * nslot
    kern = functools.partial(_attn_s_kernel, n_pages=n_pages, s_len=s_len, nslot=nslot, w2=w2, lam_init=lam_init)
    seq = lambda r: pl.BlockSpec((None, r, w2), lambda b, p, pt: (b, 0, 0))
    page = pl.BlockSpec((PAGE_SIZE * nslot, HEAD_DIM), lambda b, p, pt: (pt[b, jnp.minimum(p, n_pages - 1)], 0))
    vec = pl.BlockSpec((1, HEAD_DIM), lambda b, p, pt: (0, 0))
    grid_spec = pltpu.PrefetchScalarGridSpec(
        num_scalar_prefetch=1,
        grid=(nb, n_pages + 1),
        in_specs=[seq(s_len), seq(s_len), seq(8), seq(8), seq(8), seq(8),
                  pl.BlockSpec((None, s_len, PAGE_SIZE), lambda b, p, pt: (b, 0, p)),
                  page, page, page, page, vec, vec, vec, vec,
                  pl.BlockSpec((1, 2 * HEAD_DIM), lambda b, p, pt: (0, 0))],
        out_specs=[seq(s_len), seq(s_len)],
        scratch_shapes=[pltpu.VMEM((rows, w2), BF16), pltpu.VMEM((rows, w2), BF16),
                        pltpu.VMEM((rows, LANES), F32), pltpu.VMEM((rows, LANES), F32), pltpu.VMEM((rows, w2), F32),
                        pltpu.VMEM((rows, LANES), F32), pltpu.VMEM((rows, LANES), F32), pltpu.VMEM((rows, w2), F32)],
    )
    return pl.pallas_call(
        kern,
        grid_spec=grid_spec,
        out_shape=[jax.ShapeDtypeStruct((nb, s_len, w2), F32), jax.ShapeDtypeStruct((nb, s_len, w2), F32)],
        compiler_params=_cparams(("arbitrary", "arbitrary")),
        name="attn_sample",
    )(page_table, qa, qb, kan, van, kbn, vbn, bias, cka, cva, ckb, cvb, *lams, g_subln)


def _resid_kernel(a_ref, w_ref, x_ref, g_ref, o_ref, *acc_scr, nk, mr):
    def finish(acc):
        tm = acc.shape[0]
        if mr == 1:
            o_ref[...] = x_ref[...] + g_ref[...] * acc
        else:
            for s in range(tm // mr):
                o_ref[s * mr:(s + 1) * mr, :] = x_ref[s * mr:(s + 1) * mr, :] + g_ref[...] * acc[s * mr:(s + 1) * mr, :]

    part = jnp.dot(a_ref[...], w_ref[...], preferred_element_type=F32)
    if nk == 1:
        finish(part)
    else:
        k = pl.program_id(2)

        @pl.when(k == 0)
        def _():
            acc_scr[0][...] = part

        @pl.when((k > 0) & (k < nk - 1))
        def _():
            acc_scr[0][...] += part

        @pl.when(k == nk - 1)
        def _():
            finish(acc_scr[0][...] + part)


def _resid_mm(a, w, x, mod, gate_blk, tm, tn, tk):
    rows, kd = a.shape
    n = w.shape[1]
    mr = mod.shape[0]
    nk = kd // tk
    kern = functools.partial(_resid_kernel, nk=nk, mr=mr)
    gate_off = gate_blk * (n // tn)
    return pl.pallas_call(
        kern,
        grid=(rows // tm, n // tn, nk),
        in_specs=[pl.BlockSpec((tm, tk), lambda i, j, k: (i, k)),
                  pl.BlockSpec((tk, tn), lambda i, j, k: (k, j)),
                  pl.BlockSpec((tm, tn), lambda i, j, k: (i, j)),
                  pl.BlockSpec((mr, tn), lambda i, j, k: (0, gate_off + j))],
        out_specs=pl.BlockSpec((tm, tn), lambda i, j, k: (i, j)),
        out_shape=jax.ShapeDtypeStruct((rows, n), F32),
        scratch_shapes=[pltpu.VMEM((tm, tn), F32)] if nk > 1 else [],
        compiler_params=_cparams(("arbitrary", "arbitrary", "arbitrary")),
        name="resid_mm",
    )(a, w, x, mod)


def _up_kernel(h_ref, wg_ref, wv_ref, prev_ref, wc_ref, bc_ref, u_ref, tail_ref, carry_scr, *, hp, shift):
    i = pl.program_id(1)

    @pl.when(i == 0)
    def _():
        carry_scr[...] = prev_ref[...]

    h = h_ref[...]
    g = jnp.dot(h, wg_ref[...], preferred_element_type=F32)
    v = jnp.dot(h, wv_ref[...], preferred_element_type=F32)
    tm = g.shape[0]
    gcat = jnp.concatenate([carry_scr[...], g], axis=0)
    wc = wc_ref[...]
    conv = (bc_ref[...]
            + wc[0:1, :] * gcat[hp - 2 * shift:hp - 2 * shift + tm, :]
            + wc[1:2, :] * gcat[hp - shift:hp - shift + tm, :]
            + wc[2:3, :] * g)
    u_ref[...] = (conv * jax.nn.sigmoid(conv) * v).astype(u_ref.dtype)
    tail = g[tm - hp:, :]
    carry_scr[...] = tail
    tail_ref[...] = tail


def _up(h2, w_up16, prev, w_conv, b_conv, tm, hp, shift):
    rows, d = h2.shape
    ff = w_conv.shape[1]
    tn = _pick(ff, (256, 128))
    nj = ff // tn
    kern = functools.partial(_up_kernel, hp=hp, shift=shift)
    return pl.pallas_call(
        kern,
        grid=(nj, rows // tm),
        in_specs=[pl.BlockSpec((tm, d), lambda j, i: (i, 0)),
                  pl.BlockSpec((d, tn), lambda j, i: (0, j)),
                  pl.BlockSpec((d, tn), lambda j, i: (0, nj + j)),
                  pl.BlockSpec((hp, tn), lambda j, i: (0, j)),
                  pl.BlockSpec((CONV_W, tn), lambda j, i: (0, j)),
                  pl.BlockSpec((1, tn), lambda j, i: (0, j))],
        out_specs=[pl.BlockSpec((tm, tn), lambda j, i: (i, j)),
                   pl.BlockSpec((hp, tn), lambda j, i: (0, j))],
        out_shape=[jax.ShapeDtypeStruct((rows, ff), BF16), jax.ShapeDtypeStruct((hp, ff), F32)],
        scratch_shapes=[pltpu.VMEM((hp, tn), F32)],
        compiler_params=_cparams(("arbitrary", "arbitrary")),
        name="ffn_up",
    )(h2, w_up16, w_up16, prev, w_conv, b_conv.reshape(1, ff))


def _rope_tables(pos):
    inv = ROPE_THETA ** (-jnp.arange(0, HEAD_DIM, 2, dtype=F32) / HEAD_DIM)
    ang = pos.astype(F32)[:, None] * inv[None, :]
    cos, sin = jnp.cos(ang), jnp.sin(ang)
    return jnp.concatenate([cos, cos], axis=1), jnp.concatenate([-sin, sin], axis=1)


def _layout(d, in_width):
    w2 = d // 2
    n_idx = (in_width - 6 * w2 - IDX_DIM) // (IDX_DIM + 1)
    qi = n_idx * IDX_DIM
    tn = min(512, w2)
    assert qi % w2 == 0 and w2 % tn == 0 and n_idx <= LANES
    lay = dict(w2=w2, n_idx=n_idx, tn=tn, iq=0, qa=qi, ka=qi + w2, va=qi + 2 * w2, qb=qi + 3 * w2,
               kb=qi + 4 * w2, vb=qi + 5 * w2, sp=qi + 6 * w2, total=qi + 6 * w2 + tn)
    return lay


def _permute_w_in(w, lay):
    d = w.shape[0]
    w2, qi, n_idx = lay["w2"], lay["n_idx"] * IDX_DIM, lay["n_idx"]
    o = 0
    seg = {}
    for name, size in (("qa", w2), ("ka", w2), ("va", w2), ("iq", qi), ("ik", IDX_DIM), ("iw", n_idx),
                       ("qb", w2), ("kb", w2), ("vb", w2)):
        seg[name] = w[:, o:o + size]
        o += size
    pad = jnp.zeros((d, lay["tn"] - IDX_DIM - n_idx), w.dtype)
    cols = [seg[k] for k in ("iq", "qa", "ka", "va", "qb", "kb", "vb", "ik", "iw")] + [pad]
    return jnp.concatenate(cols, axis=1).astype(BF16)


def _layer(x2d, mod, pos, prev, attend, lw, lay, tm_rows, tm_norm, hp, shift):
    g_attn, w_perm, w_o16, g_ffn, w_up16, w_conv, b_conv, w_down16, g_final = lw
    rows, d = x2d.shape
    cosf, sinf = _rope_tables(pos)
    h = _norm_mod(x2d, g_attn, mod, 1, 0, tm_norm)
    p32, p16 = _proj(h, w_perm, cosf, sinf, lay, tm_rows)
    o16 = attend(p32, p16)
    tn = _pick(d, (512, 256, 128))
    x1 = _resid_mm(o16, w_o16, x2d, mod, 2, tm_rows, tn, d)
    h2 = _norm_mod(x1, g_ffn, mod, 4, 3, tm_norm)
    u, tail = _up(h2, w_up16, prev, w_conv, b_conv, tm_rows, hp, shift)
    ff = u.shape[1]
    tk = ff // 2 if (ff // 2) % LANES == 0 else ff
    x2 = _resid_mm(u, w_down16, x1, mod, 5, min(tm_rows, 512), tn, tk)
    y = _final_norm(x2, g_final, tm_norm)
    return y, p32, tail


def kernel(x_prompt, x_sample, cache_k_a, cache_v_a, cache_k_idx, cache_k_b, cache_v_b, state_conv, page_table,
           c_prompt, c_sample, w_ada, b_ada, g_attn, w_in, lam_q1, lam_k1, lam_q2, lam_k2, g_subln, w_o, g_ffn,
           w_up, w_conv, b_conv, w_down, g_final):
    depth = w_ada.shape[0]
    assert depth == 1 and x_prompt.shape[0] == 1
    bsz, t_len, d = x_prompt.shape
    nb, s_len, _ = x_sample.shape
    n_pages = page_table.shape[1]
    past_len = n_pages * PAGE_SIZE
    ff = w_conv.shape[-1]
    lay = _layout(d, w_in.shape[-1])
    w2, n_idx = lay["w2"], lay["n_idx"]
    nha, nhb = w2 // HEAD_DIM, w2 // (2 * HEAD_DIM)
    l = 0
    lam_init = 0.8 - 0.6 * math.exp(-0.3 * l)

    w_perm = _permute_w_in(w_in[l], lay)
    lw = (g_attn[l], w_perm, w_o[l].astype(BF16), g_ffn[l], w_up[l].astype(BF16), w_conv[l], b_conv[l],
          w_down[l].astype(BF16), g_final)
    lams = tuple(a[l].reshape(1, HEAD_DIM) for a in (lam_q1, lam_k1, lam_q2, lam_k2))
    gs = g_subln[l].reshape(1, 2 * HEAD_DIM)

    n_c = nb + 16
    c_all = jnp.concatenate([c_sample, c_prompt, jnp.zeros((n_c - nb - 1, d), F32)], axis=0).astype(BF16)
    mod = _ada(c_all, w_ada[l], b_ada[l])
    mod_s, mod_p = mod[:nb], mod[nb:nb + 1]

    tq = _pick(t_len, (256, 128))
    tk = _pick(t_len, (512, 256, 128))
    k_sel_p = min(TOPK_MAX, t_len // 4)

    def attend_p(p32, p16):
        bias = _score_p(p16, p32, lay, tq, tk, k_sel_p)
        oa = _sattn_p(p16, bias, lay, tq, tk)
        ob = _dattn_p(p16, lams, gs, lay, tq, tk, lam_init)
        return jnp.concatenate([oa, ob], axis=1)

    tm_p = _pick(t_len, (1024, 512, 256, 128))
    y_p, p32_p, tail_p = _layer(x_prompt[0], mod_p, jnp.arange(t_len), jnp.zeros((8, ff), F32), attend_p, lw, lay,
                                tm_p, _pick(t_len, (512, 256, 128)), 8, 1)

    def seg(p32, name, width):
        return p32[:, lay[name]:lay[name] + width]

    outs_p = (
        y_p.reshape(1, t_len, d),
        seg(p32_p, "ka", w2).reshape(1, 1, t_len, nha, HEAD_DIM),
        seg(p32_p, "va", w2).reshape(1, 1, t_len, nha, HEAD_DIM),
        seg(p32_p, "sp", IDX_DIM).reshape(1, 1, t_len, IDX_DIM),
        seg(p32_p, "kb", w2).reshape(1, 1, t_len, nhb, 2, HEAD_DIM),
        seg(p32_p, "vb", w2).reshape(1, 1, t_len, nhb, 2 * HEAD_DIM),
        tail_p[8 - (CONV_W - 1):].reshape(1, 1, CONV_W - 1, ff),
    )

    rows_s = s_len * nb
    xs_tm = jnp.transpose(x_sample, (1, 0, 2)).reshape(rows_s, d)
    pos_s = past_len + jnp.repeat(jnp.arange(s_len), nb)
    prev_s = jnp.transpose(state_conv[l], (1, 0, 2)).reshape((CONV_W - 1) * nb, ff)
    k_sel_s = min(TOPK_MAX, (past_len + s_len) // 4)
    n_phys = cache_k_a.shape[1]

    def to_bm(a):
        return jnp.transpose(a.reshape(s_len, nb, a.shape[1]), (1, 0, 2))

    def pad8(a):
        return jnp.pad(a, ((0, 0), (0, 8 - s_len), (0, 0)))

    bm_cache = {}
    flat = lambda c: c.reshape(n_phys * PAGE_SIZE * nha, HEAD_DIM)

    def attend_s(p32, p16):
        bm = to_bm(p32)
        bm_cache["bm"] = bm
        sl = lambda name, width: bm[:, :, lay[name]:lay[name] + width]
        iq = sl("iq", n_idx * IDX_DIM).reshape(nb, s_len * n_idx, IDX_DIM)
        iw = bm[:, :, lay["sp"] + IDX_DIM:lay["sp"] + IDX_DIM + n_idx].reshape(nb, s_len * n_idx, 1)
        scores = _score_s(page_table, iq, iw, pad8(sl("sp", IDX_DIM)),
                          cache_k_idx[l].reshape(n_phys, PAGE_SIZE, IDX_DIM), past_len, s_len)
        l_pad = scores.shape[2]
        bias = _thresh(scores.reshape(nb * s_len, l_pad), k_sel_s).reshape(nb, s_len, l_pad)
        oa, ob = _attn_s(page_table, sl("qa", w2), sl("qb", w2), pad8(sl("ka", w2)), pad8(sl("va", w2)),
                         pad8(sl("kb", w2)), pad8(sl("vb", w2)), bias,
                         flat(cache_k_a[l]), flat(cache_v_a[l]), flat(cache_k_b[l]),
                         flat(jnp.swapaxes(cache_v_b[l].reshape(n_phys, PAGE_SIZE, nhb, 2, HEAD_DIM), 2, 3)),
                         lams, gs, lam_init)
        o = jnp.concatenate([oa, ob], axis=2)
        return jnp.transpose(o, (1, 0, 2)).reshape(rows_s, d).astype(BF16)

    y_s, _, tail_s = _layer(xs_tm, mod_s, pos_s, prev_s, attend_s, lw, lay, rows_s, nb, (CONV_W - 1) * nb, nb)
    bm = bm_cache["bm"]
    sb = lambda name, width: bm[:, :, lay[name]:lay[name] + width]
    outs_s = (
        jnp.transpose(y_s.reshape(s_len, nb, d), (1, 0, 2)),
        sb("ka", w2).reshape(1, nb, s_len, nha, HEAD_DIM),
        sb("va", w2).reshape(1, nb, s_len, nha, HEAD_DIM),
        sb("sp", IDX_DIM).reshape(1, nb, s_len, IDX_DIM),
        sb("kb", w2).reshape(1, nb, s_len, nhb, 2, HEAD_DIM),
        sb("vb", w2).reshape(1, nb, s_len, nhb, 2 * HEAD_DIM),
        jnp.transpose(tail_s.reshape(CONV_W - 1, nb, ff), (1, 0, 2)).reshape(1, nb, CONV_W - 1, ff),
    )
    return (outs_p[0], outs_s[0]) + outs_p[1:] + outs_s[1:]
```

```python
import functools
import math

import jax
import jax.numpy as jnp
from jax import lax
from jax.experimental import pallas as pl
from jax.experimental.pallas import tpu as pltpu

F32 = jnp.float32
BF16 = jnp.bfloat16

HEAD_DIM = 128
IDX_DIM = 128
PAGE_SIZE = 128
TOPK_MAX = 256
CONV_W = 3
ROPE_THETA = 10000.0
EPS = 1e-6
HEAD_SCALE = HEAD_DIM ** -0.5
SCORE_SCALE = HEAD_SCALE * math.log2(math.e)
IDX_SCALE = IDX_DIM ** -0.5
LANES = 128
NEG = -1e30
INT_MIN = -(2 ** 31)
KEY_NEG_INF = INT_MIN + 0x7FFFFF
VMEM_LIMIT = 56 * 1024 * 1024


def _cparams(sem):
    return pltpu.CompilerParams(dimension_semantics=sem, vmem_limit_bytes=VMEM_LIMIT)


def _pick(n, cands):
    for c in cands:
        if n % c == 0:
            return c
    return n


def _ada_kernel(c_ref, w_ref, b_ref, o_ref):
    acc = jnp.dot(c_ref[...], w_ref[...].astype(BF16), preferred_element_type=F32)
    o_ref[...] = acc + b_ref[...]


def _ada(c_all, w_ada, b_ada):
    r, d = c_all.shape
    n = w_ada.shape[1]
    tn = _pick(n, (512, 256, 128))
    return pl.pallas_call(
        _ada_kernel,
        grid=(n // tn,),
        in_specs=[pl.BlockSpec((r, d), lambda j: (0, 0)),
                  pl.BlockSpec((d, tn), lambda j: (0, j)),
                  pl.BlockSpec((1, tn), lambda j: (0, j))],
        out_specs=pl.BlockSpec((r, tn), lambda j: (0, j)),
        out_shape=jax.ShapeDtypeStruct((r, n), F32),
        compiler_params=_cparams(("arbitrary",)),
        name="ada",
    )(c_all, w_ada, b_ada.reshape(1, n))


def _norm_mod_kernel(x_ref, g_ref, sc_ref, sh_ref, o_ref):
    x = x_ref[...]
    y = x * lax.rsqrt(jnp.mean(x * x, axis=-1, keepdims=True) + EPS) * g_ref[...]
    o_ref[...] = (y * (1.0 + sc_ref[...]) + sh_ref[...]).astype(o_ref.dtype)


def _norm_mod(x, g, mod, sc_blk, sh_blk, tm):
    rows, d = x.shape
    mr = mod.shape[0]
    assert mr == 1 or mr == tm
    return pl.pallas_call(
        _norm_mod_kernel,
        grid=(rows // tm,),
        in_specs=[pl.BlockSpec((tm, d), lambda i: (i, 0)),
                  pl.BlockSpec((1, d), lambda i: (0, 0)),
                  pl.BlockSpec((mr, d), lambda i: (0, sc_blk)),
                  pl.BlockSpec((mr, d), lambda i: (0, sh_blk))],
        out_specs=pl.BlockSpec((tm, d), lambda i: (i, 0)),
        out_shape=jax.ShapeDtypeStruct((rows, d), BF16),
        compiler_params=_cparams(("arbitrary",)),
        name="norm_mod",
    )(x, g.reshape(1, d), mod, mod)


def _final_norm_kernel(x_ref, g_ref, o_ref):
    x = x_ref[...]
    o_ref[...] = x * lax.rsqrt(jnp.mean(x * x, axis=-1, keepdims=True) + EPS) * g_ref[...]


def _final_norm(x, g, tm):
    rows, d = x.shape
    return pl.pallas_call(
        _final_norm_kernel,
        grid=(rows // tm,),
        in_specs=[pl.BlockSpec((tm, d), lambda i: (i, 0)),
                  pl.BlockSpec((1, d), lambda i: (0, 0))],
        out_specs=pl.BlockSpec((tm, d), lambda i: (i, 0)),
        out_shape=jax.ShapeDtypeStruct((rows, d), F32),
        compiler_params=_cparams(("arbitrary",)),
        name="final_norm",
    )(x, g.reshape(1, d))


def _rope_cols(a, cos, sin):
    outs = []
    for c in range(a.shape[1] // HEAD_DIM):
        xh = a[:, c * HEAD_DIM:(c + 1) * HEAD_DIM]
        outs.append(xh * cos + pltpu.roll(xh, HEAD_DIM // 2, 1) * sin)
    return outs[0] if len(outs) == 1 else jnp.concatenate(outs, axis=1)


def _proj_kernel(h_ref, wn_ref, wt_ref, cos_ref, sin_ref, o32_ref, o16_ref, acc_scr, *, n_reg, v_lo, v_hi, iw_scale):
    j = pl.program_id(1)
    jt = j - n_reg - 1

    @pl.when(j <= n_reg)
    def _():
        acc_scr[...] = jnp.dot(h_ref[...], wn_ref[...], preferred_element_type=F32)

    @pl.when(j > n_reg)
    def _():
        acc_scr[...] = jnp.dot(h_ref[...], wt_ref[...], preferred_element_type=F32)

    acc = acc_scr[...]
    is_plain = ((j >= v_lo) & (j < v_hi)) | ((jt >= v_lo) & (jt < v_hi))
    is_special = j == n_reg

    def emit(v):
        o32_ref[...] = v
        o16_ref[...] = v.astype(BF16)

    @pl.when(is_plain)
    def _():
        emit(acc)

    @pl.when(is_special)
    def _():
        parts = [_rope_cols(acc[:, :IDX_DIM], cos_ref[...], sin_ref[...]), acc[:, IDX_DIM:2 * IDX_DIM] * iw_scale]
        if acc.shape[1] > 2 * IDX_DIM:
            parts.append(acc[:, 2 * IDX_DIM:])
        emit(jnp.concatenate(parts, axis=1))

    @pl.when(jnp.logical_not(is_plain | is_special))
    def _():
        emit(_rope_cols(acc, cos_ref[...], sin_ref[...]))


def _proj(h, w_nat, w_tail, cosf, sinf, lay, tm):
    rows, d = h.shape
    tn, w2 = lay["tn"], lay["w2"]
    n_a, n_b = 3 * w2 // tn, lay["qa"] // tn
    n_reg = n_a + n_b
    n = lay["total"]
    assert (n_reg + 1) * tn <= w_nat.shape[1] and w_tail.shape[1] == n_a * tn

    def out_blk(j):
        return jnp.where(j < n_a, j + n_b, jnp.where(j < n_reg, j - n_a, jnp.where(j == n_reg, n_reg + n_a, j - 1)))

    kern = functools.partial(_proj_kernel, n_reg=n_reg, v_lo=2 * w2 // tn, v_hi=n_a, iw_scale=lay["n_idx"] ** -0.5)
    return pl.pallas_call(
        kern,
        grid=(rows // tm, n // tn),
        in_specs=[pl.BlockSpec((tm, d), lambda i, j: (i, 0)),
                  pl.BlockSpec((d, tn), lambda i, j: (0, jnp.minimum(j, n_reg))),
                  pl.BlockSpec((d, tn), lambda i, j: (0, jnp.maximum(j - n_reg - 1, 0))),
                  pl.BlockSpec((tm, HEAD_DIM), lambda i, j: (i, 0)),
                  pl.BlockSpec((tm, HEAD_DIM), lambda i, j: (i, 0))],
        out_specs=[pl.BlockSpec((tm, tn), lambda i, j: (i, out_blk(j))),
                   pl.BlockSpec((tm, tn), lambda i, j: (i, out_blk(j)))],
        out_shape=[jax.ShapeDtypeStruct((rows, n), F32), jax.ShapeDtypeStruct((rows, n), BF16)],
        scratch_shapes=[pltpu.VMEM((tm, tn), F32)],
        compiler_params=_cparams(("arbitrary", "arbitrary")),
        name="proj",
    )(h, w_nat, w_tail, cosf, sinf)


def _float_key(x):
    bits = pltpu.bitcast(x, jnp.int32)
    return bits ^ ((bits >> 31) & 0x7FFFFFFF)


def _topk_bias(key_ref, thr_ref, o_ref, *, rows, nch, total_ch, k, rg, unroll):
    for g in range(rows // rg):
        r0 = g * rg

        def bit_body(b, t, r0=r0):
            cand = t + lax.shift_left(jnp.int32(1), jnp.int32(31) - b)

            def ch_body(c, cnt):
                for u in range(unroll):
                    sl = pl.ds(pl.multiple_of((c * unroll + u) * LANES, LANES), LANES)
                    cnt = cnt + jnp.where(key_ref[r0:r0 + rg, sl] >= cand, 1.0, 0.0)
                return cnt

            cnt = lax.fori_loop(0, nch // unroll, ch_body, jnp.zeros((rg, LANES), F32))
            tot = jnp.sum(cnt, axis=1, keepdims=True)
            return jnp.where(tot >= float(k), cand, t)

        t = lax.fori_loop(0, 32, bit_body, jnp.full((rg, LANES), INT_MIN, jnp.int32))
        thr_ref[r0:r0 + rg, :] = jnp.maximum(t, KEY_NEG_INF + 1)

    thr = thr_ref[...]

    def out_body(c, carry):
        sl = pl.ds(pl.multiple_of(c * LANES, LANES), LANES)
        o_ref[:, sl] = jnp.where(key_ref[:, sl] >= thr, 0.0, NEG).astype(o_ref.dtype)
        return carry

    def fill_body(c, carry):
        sl = pl.ds(pl.multiple_of(c * LANES, LANES), LANES)
        o_ref[:, sl] = jnp.full((rows, LANES), NEG, o_ref.dtype)
        return carry

    lax.fori_loop(0, nch, out_body, 0)
    lax.fori_loop(nch, total_ch, fill_body, 0)


def _score_p_kernel(iq_ref, ik_ref, iw_ref, o_ref, key_scr, wb_scr, thr_scr, *, tq, tk, n_idx, k_sel, nj, t_len):
    i = pl.program_id(0)
    j = pl.program_id(1)
    jmax = ((i + 1) * tq - 1) // tk

    @pl.when(j == 0)
    def _():
        w = iw_ref[...] * IDX_SCALE
        for h in range(n_idx):
            wb_scr[h] = jnp.broadcast_to(w[:, h:h + 1], (tq, LANES))

    @pl.when(j <= jmax)
    def _():
        kb = ik_ref[...]
        acc = jnp.zeros((tq, tk), F32)
        for h in range(n_idx):
            s = lax.dot_general(iq_ref[:, h * IDX_DIM:(h + 1) * IDX_DIM], kb, (((1,), (1,)), ((), ())),
                                preferred_element_type=F32)
            acc = acc + jnp.maximum(s, 0.0) * pltpu.repeat(wb_scr[h], tk // LANES, axis=1)
        rows = i * tq + lax.broadcasted_iota(jnp.int32, (tq, tk), 0)
        cols = j * tk + lax.broadcasted_iota(jnp.int32, (tq, tk), 1)
        acc = jnp.where(cols <= rows, acc, -jnp.inf)
        key_scr[:, pl.ds(pl.multiple_of(j * tk, tk), tk)] = _float_key(acc)

    @pl.when(j == nj - 1)
    def _():
        _topk_bias(key_scr, thr_scr, o_ref, rows=tq, nch=(jmax + 1) * (tk // LANES), total_ch=t_len // LANES,
                   k=k_sel, rg=min(tq, 128), unroll=tk // LANES)


def _score_p(p16, p32, lay, tq, tk, k_sel):
    t_len = p16.shape[0]
    n_idx = lay["n_idx"]
    qi = n_idx * IDX_DIM
    ni, nj = t_len // tq, t_len // tk
    sp_blk = lay["sp"] // IDX_DIM
    kern = functools.partial(_score_p_kernel, tq=tq, tk=tk, n_idx=n_idx, k_sel=k_sel, nj=nj, t_len=t_len)
    return pl.pallas_call(
        kern,
        grid=(ni, nj),
        in_specs=[pl.BlockSpec((tq, qi), lambda i, j: (i, 0)),
                  pl.BlockSpec((tk, IDX_DIM), lambda i, j: (jnp.minimum(j, ((i + 1) * tq - 1) // tk), sp_blk)),
                  pl.BlockSpec((tq, IDX_DIM), lambda i, j: (i, sp_blk + 1))],
        out_specs=pl.BlockSpec((tq, t_len), lambda i, j: (i, 0)),
        out_shape=jax.ShapeDtypeStruct((t_len, t_len), BF16),
        scratch_shapes=[pltpu.VMEM((tq, t_len), jnp.int32),
                        pltpu.VMEM((n_idx, tq, LANES), F32),
                        pltpu.VMEM((tq, LANES), jnp.int32)],
        compiler_params=_cparams(("arbitrary", "arbitrary")),
        name="score_prompt",
    )(p16, p16, p32)


def _thresh_kernel(s_ref, o_ref, key_scr, thr_scr, *, rows, nch, k_sel):
    key_scr[...] = _float_key(s_ref[...])
    _topk_bias(key_scr, thr_scr, o_ref, rows=rows, nch=nch, total_ch=nch, k=k_sel, rg=min(rows, 64), unroll=nch)


def _thresh(s, k_sel):
    rows, l = s.shape
    kern = functools.partial(_thresh_kernel, rows=rows, nch=l // LANES, k_sel=k_sel)
    return pl.pallas_call(
        kern,
        grid=(1,),
        in_specs=[pl.BlockSpec((rows, l), lambda i: (0, 0))],
        out_specs=pl.BlockSpec((rows, l), lambda i: (0, 0)),
        out_shape=jax.ShapeDtypeStruct((rows, l), F32),
        scratch_shapes=[pltpu.VMEM((rows, l), jnp.int32), pltpu.VMEM((rows, LANES), jnp.int32)],
        compiler_params=_cparams(("arbitrary",)),
        name="thresh_sample",
    )(s)


def _online_step(s, v, m_scr, l_scr, hd, acc_ref, c0, c1):
    tk = s.shape[1]
    m_prev = m_scr[hd]
    l_prev = l_scr[hd]
    m_new = jnp.maximum(m_prev, jnp.max(s, axis=1, keepdims=True))
    alpha = jnp.exp2(m_prev - m_new)
    p = jnp.exp2(s - pltpu.repeat(m_new, tk // LANES, axis=1))
    l_scr[hd] = alpha * l_prev + jnp.sum(p, axis=1, keepdims=True)
    m_scr[hd] = m_new
    pv = jnp.dot(p.astype(BF16), v, preferred_element_type=F32)
    arep = alpha if (c1 - c0) == LANES else pltpu.repeat(alpha, (c1 - c0) // LANES, axis=1)
    acc_ref[:, c0:c1] = acc_ref[:, c0:c1] * arep + pv


def _qk(q, k):
    return lax.dot_general(q, k, (((1,), (1,)), ((), ())), preferred_element_type=F32)


def _sattn_p_kernel(q_ref, k_ref, v_ref, b_ref, o_ref, m_scr, l_scr, acc_scr, *, tq, tk, nh, nj):
    i = pl.program_id(0)
    j = pl.program_id(1)
    jmax = ((i + 1) * tq - 1) // tk

    @pl.when(j == 0)
    def _():
        m_scr[...] = jnp.full(m_scr.shape, NEG, F32)
        l_scr[...] = jnp.zeros(l_scr.shape, F32)
        acc_scr[...] = jnp.zeros(acc_scr.shape, F32)

    @pl.when(j <= jmax)
    def _():
        bias = b_ref[...].astype(F32)
        for h in range(nh):
            c0, c1 = h * HEAD_DIM, (h + 1) * HEAD_DIM
            s = _qk(q_ref[:, c0:c1], k_ref[:, c0:c1]) * SCORE_SCALE + bias
            _online_step(s, v_ref[:, c0:c1], m_scr, l_scr, h, acc_scr, c0, c1)

    @pl.when(j == nj - 1)
    def _():
        for h in range(nh):
            c0, c1 = h * HEAD_DIM, (h + 1) * HEAD_DIM
            o_ref[:, c0:c1] = (acc_scr[:, c0:c1] / l_scr[h]).astype(o_ref.dtype)


def _sattn_p(p16, bias, lay, tq, tk):
    t_len = p16.shape[0]
    w2 = lay["w2"]
    nh = w2 // HEAD_DIM
    ni, nj = t_len // tq, t_len // tk
    jm = lambda i, j: jnp.minimum(j, ((i + 1) * tq - 1) // tk)
    kern = functools.partial(_sattn_p_kernel, tq=tq, tk=tk, nh=nh, nj=nj)
    return pl.pallas_call(
        kern,
        grid=(ni, nj),
        in_specs=[pl.BlockSpec((tq, w2), lambda i, j: (i, lay["qa"] // w2)),
                  pl.BlockSpec((tk, w2), lambda i, j: (jm(i, j), lay["ka"] // w2)),
                  pl.BlockSpec((tk, w2), lambda i, j: (jm(i, j), lay["va"] // w2)),
                  pl.BlockSpec((tq, tk), lambda i, j: (i, jm(i, j)))],
        out_specs=pl.BlockSpec((tq, w2), lambda i, j: (i, 0)),
        out_shape=jax.ShapeDtypeStruct((t_len, w2), BF16),
        scratch_shapes=[pltpu.VMEM((nh, tq, LANES), F32), pltpu.VMEM((nh, tq, LANES), F32),
                        pltpu.VMEM((tq, w2), F32)],
        compiler_params=_cparams(("arbitrary", "arbitrary")),
        name="sparse_attn_prompt",
    )(p16, p16, p16, bias)


def _lam_value(lq1, lk1, lq2, lk2, lam_init):
    a = jnp.sum(lq1[...] * lk1[...], axis=1, keepdims=True)
    b = jnp.sum(lq2[...] * lk2[...], axis=1, keepdims=True)
    return jnp.exp(a) - jnp.exp(b) + lam_init


def _subln(o, g, lam_init):
    return o * lax.rsqrt(jnp.mean(o * o, axis=1, keepdims=True) + EPS) * g * (1.0 - lam_init)


def _dattn_p_kernel(q_ref, k_ref, v_ref, lq1, lk1, lq2, lk2, g_ref, o_ref, m_scr, l_scr, acc0_scr, acc1_scr,
                    *, tq, tk, nhb, nj, lam_init):
    i = pl.program_id(0)
    j = pl.program_id(1)
    jmax = ((i + 1) * tq - 1) // tk
    dv = 2 * HEAD_DIM

    @pl.when(j == 0)
    def _():
        m_scr[...] = jnp.full(m_scr.shape, NEG, F32)
        l_scr[...] = jnp.zeros(l_scr.shape, F32)
        acc0_scr[...] = jnp.zeros(acc0_scr.shape, F32)
        acc1_scr[...] = jnp.zeros(acc1_scr.shape, F32)

    @pl.when(j <= jmax)
    def _():
        rows = i * tq + lax.broadcasted_iota(jnp.int32, (tq, tk), 0)
        cols = j * tk + lax.broadcasted_iota(jnp.int32, (tq, tk), 1)
        bias = jnp.where(cols <= rows, 0.0, NEG)
        for h in range(nhb):
            v = v_ref[:, h * dv:(h + 1) * dv]
            for c, acc in ((0, acc0_scr), (1, acc1_scr)):
                hd = 2 * h + c
                c0, c1 = hd * HEAD_DIM, (hd + 1) * HEAD_DIM
                s = _qk(q_ref[:, c0:c1], k_ref[:, c0:c1]) * SCORE_SCALE + bias
                _online_step(s, v, m_scr, l_scr, hd, acc, h * dv, (h + 1) * dv)

    @pl.when(j == nj - 1)
    def _():
        lam = _lam_value(lq1, lk1, lq2, lk2, lam_init)
        g = g_ref[...]
        for h in range(nhb):
            c0, c1 = h * dv, (h + 1) * dv
            o0 = acc0_scr[:, c0:c1] / pltpu.repeat(l_scr[2 * h], 2, axis=1)
            o1 = acc1_scr[:, c0:c1] / pltpu.repeat(l_scr[2 * h + 1], 2, axis=1)
            o_ref[:, c0:c1] = _subln(o0 - lam * o1, g, lam_init).astype(o_ref.dtype)


def _dattn_p(p16, lams, g_subln, lay, tq, tk, lam_init):
    t_len = p16.shape[0]
    w2 = lay["w2"]
    nhb = w2 // (2 * HEAD_DIM)
    ni, nj = t_len // tq, t_len // tk
    jm = lambda i, j: jnp.minimum(j, ((i + 1) * tq - 1) // tk)
    kern = functools.partial(_dattn_p_kernel, tq=tq, tk=tk, nhb=nhb, nj=nj, lam_init=lam_init)
    vec = pl.BlockSpec((1, HEAD_DIM), lambda i, j: (0, 0))
    return pl.pallas_call(
        kern,
        grid=(ni, nj),
        in_specs=[pl.BlockSpec((tq, w2), lambda i, j: (i, lay["qb"] // w2)),
                  pl.BlockSpec((tk, w2), lambda i, j: (jm(i, j), lay["kb"] // w2)),
                  pl.BlockSpec((tk, w2), lambda i, j: (jm(i, j), lay["vb"] // w2)),
                  vec, vec, vec, vec,
                  pl.BlockSpec((1, 2 * HEAD_DIM), lambda i, j: (0, 0))],
        out_specs=pl.BlockSpec((tq, w2), lambda i, j: (i, 0)),
        out_shape=jax.ShapeDtypeStruct((t_len, w2), BF16),
        scratch_shapes=[pltpu.VMEM((2 * nhb, tq, LANES), F32), pltpu.VMEM((2 * nhb, tq, LANES), F32),
                        pltpu.VMEM((tq, w2), F32), pltpu.VMEM((tq, w2), F32)],
        compiler_params=_cparams(("arbitrary", "arbitrary")),
        name="diff_attn_prompt",
    )(p16, p16, p16, *lams, g_subln)


def _score_s_kernel(pt_ref, iq_ref, iw_ref, iknew_ref, *rest, n_pages, s_len, n_idx, past_len):
    page_refs = rest[:n_pages]
    o_ref = rest[n_pages]
    q = iq_ref[...].astype(BF16)
    w = iw_ref[...] * IDX_SCALE
    knew = jnp.concatenate([iknew_ref[...], jnp.zeros((PAGE_SIZE - 8, IDX_DIM), F32)], axis=0)
    blocks = [r[...] for r in page_refs] + [knew]
    for p, kb in enumerate(blocks):
        s = _qk(q, kb.astype(BF16))
        s = jnp.maximum(s, 0.0) * w
        sc = jnp.sum(s.reshape(s_len, n_idx, PAGE_SIZE), axis=1)
        cols = p * PAGE_SIZE + lax.broadcasted_iota(jnp.int32, (s_len, PAGE_SIZE), 1)
        qpos = past_len + lax.broadcasted_iota(jnp.int32, (s_len, PAGE_SIZE), 0)
        o_ref[:, p * PAGE_SIZE:(p + 1) * PAGE_SIZE] = jnp.where(cols <= qpos, sc, -jnp.inf)


def _score_s(page_table, iq_s, iw_s, iknew8, ck_idx, past_len, s_len):
    nb, rows, _ = iq_s.shape
    n_pages = page_table.shape[1]
    n_idx = rows // s_len
    l_pad = (n_pages + 1) * PAGE_SIZE
    kern = functools.partial(_score_s_kernel, n_pages=n_pages, s_len=s_len, n_idx=n_idx, past_len=past_len)
    page_specs = [pl.BlockSpec((None, PAGE_SIZE, IDX_DIM), functools.partial(lambda b, pt, p: (pt[b, p], 0, 0), p=p))
                  for p in range(n_pages)]
    grid_spec = pltpu.PrefetchScalarGridSpec(
        num_scalar_prefetch=1,
        grid=(nb,),
        in_specs=[pl.BlockSpec((None, rows, IDX_DIM), lambda b, pt: (b, 0, 0)),
                  pl.BlockSpec((None, rows, 1), lambda b, pt: (b, 0, 0)),
                  pl.BlockSpec((None, 8, IDX_DIM), lambda b, pt: (b, 0, 0))] + page_specs,
        out_specs=pl.BlockSpec((None, s_len, l_pad), lambda b, pt: (b, 0, 0)),
    )
    return pl.pallas_call(
        kern,
        grid_spec=grid_spec,
        out_shape=jax.ShapeDtypeStruct((nb, s_len, l_pad), F32),
        compiler_params=_cparams(("arbitrary",)),
        name="score_sample",
    )(page_table, iq_s, iw_s, iknew8, *([ck_idx] * n_pages))


def _blockdiag_q(q, n_slots, width):
    s_len, w = q.shape
    rep = jnp.broadcast_to(q[:, None, :], (s_len, n_slots, w)).reshape(s_len * n_slots, w)
    return jnp.where(_diag_mask(s_len, n_slots, w, width), rep, 0.0)


def _diag_mask(s_len, n_slots, w, width):
    r = lax.broadcasted_iota(jnp.int32, (s_len * n_slots, w), 0) % n_slots
    c = lax.broadcasted_iota(jnp.int32, (s_len * n_slots, w), 1) // width
    return r == c


def _page_step(qbd_scr, k, v, bias, m_scr, l_scr, acc_scr):
    s = _qk(qbd_scr[...], k) * SCORE_SCALE + bias
    m_prev = m_scr[...]
    m_new = jnp.maximum(m_prev, jnp.max(s, axis=1, keepdims=True))
    alpha = jnp.exp2(m_prev - m_new)
    nrep = s.shape[1] // LANES
    p = jnp.exp2(s - (m_new if nrep == 1 else pltpu.repeat(m_new, nrep, axis=1)))
    l_scr[...] = alpha * l_scr[...] + jnp.sum(p, axis=1, keepdims=True)
    m_scr[...] = m_new
    pv = jnp.dot(p.astype(BF16), v, preferred_element_type=F32)
    acc_scr[...] = acc_scr[...] * pltpu.repeat(alpha, acc_scr.shape[1] // LANES, axis=1) + pv


def _attn_s_kernel(pt_ref, qa_ref, qb_ref, kan_ref, van_ref, kbn_ref, vbn_ref, bias_ref, biasn_ref, *rest,
                   n_steps, gp, s_len, nslot, w2, lam_init):
    cka_refs, cva_refs, ckb_refs, cvb_refs = (rest[c * gp:(c + 1) * gp] for c in range(4))
    (lq1, lk1, lq2, lk2, g_ref, oa_ref, ob_ref,
     qa_scr, qb_scr, ma_scr, la_scr, acca_scr, mb_scr, lb_scr, accb_scr) = rest[4 * gp:]
    p = pl.program_id(1)
    rows = s_len * nslot

    @pl.when(p == 0)
    def _():
        qa_scr[...] = _blockdiag_q(qa_ref[...], nslot, HEAD_DIM).astype(BF16)
        qb_scr[...] = _blockdiag_q(qb_ref[...], nslot, HEAD_DIM).astype(BF16)
        for m_scr, l_scr, acc_scr in ((ma_scr, la_scr, acca_scr), (mb_scr, lb_scr, accb_scr)):
            m_scr[...] = jnp.full(m_scr.shape, NEG, F32)
            l_scr[...] = jnp.zeros(l_scr.shape, F32)
            acc_scr[...] = jnp.zeros(acc_scr.shape, F32)

    def bias_rows(b4):
        return jnp.broadcast_to(b4[:, None, :], (s_len, nslot, b4.shape[1])).reshape(rows, b4.shape[1])

    def load_page(ref, starts):
        return jnp.concatenate([ref[pl.ds(s, PAGE_SIZE, stride=nslot), :].astype(BF16) for s in starts], axis=1)

    def load_pages(refs, starts):
        pages = [load_page(r, starts) for r in refs]
        return pages[0] if len(pages) == 1 else jnp.concatenate(pages, axis=0)

    slots = tuple(range(nslot))
    vb_slots = tuple((s % 2) * (nslot // 2) + s // 2 for s in slots)

    @pl.when(p < n_steps)
    def _():
        _page_step(qa_scr, load_pages(cka_refs, slots), load_pages(cva_refs, slots), bias_rows(bias_ref[...]),
                   ma_scr, la_scr, acca_scr)
        _page_step(qb_scr, load_pages(ckb_refs, slots), load_pages(cvb_refs, vb_slots), 0.0,
                   mb_scr, lb_scr, accb_scr)

    @pl.when(p == n_steps)
    def _():
        pad = lambda r: jnp.concatenate([r[...], jnp.zeros((PAGE_SIZE - 8, w2), F32)], axis=0).astype(BF16)
        _page_step(qa_scr, pad(kan_ref), pad(van_ref), bias_rows(biasn_ref[...]), ma_scr, la_scr, acca_scr)
        kcol = lax.broadcasted_iota(jnp.int32, (s_len, PAGE_SIZE), 1)
        qrow = lax.broadcasted_iota(jnp.int32, (s_len, PAGE_SIZE), 0)
        causal = jnp.where(kcol <= qrow, 0.0, NEG)
        _page_step(qb_scr, pad(kbn_ref), pad(vbn_ref), bias_rows(causal), mb_scr, lb_scr, accb_scr)

        oa = jnp.where(_diag_mask(s_len, nslot, w2, HEAD_DIM), acca_scr[...] / la_scr[...][:, :1], 0.0)
        oa_ref[...] = jnp.sum(oa.reshape(s_len, nslot, w2), axis=1)

        nb = accb_scr[...] / lb_scr[...][:, :1]
        r = lax.broadcasted_iota(jnp.int32, (rows, w2), 0) % nslot
        c = lax.broadcasted_iota(jnp.int32, (rows, w2), 1) // (2 * HEAD_DIM)
        own = (r // 2) == c
        o0 = jnp.sum(jnp.where(own & (r % 2 == 0), nb, 0.0).reshape(s_len, nslot, w2), axis=1)
        o1 = jnp.sum(jnp.where(own & (r % 2 == 1), nb, 0.0).reshape(s_len, nslot, w2), axis=1)
        ob = o0 - _lam_value(lq1, lk1, lq2, lk2, lam_init) * o1
        g = g_ref[...]
        dv = 2 * HEAD_DIM
        for h in range(w2 // dv):
            ob_ref[:, h * dv:(h + 1) * dv] = _subln(ob[:, h * dv:(h + 1) * dv], g, lam_init)


def _attn_s(page_table, qa, qb, kan, van, kbn, vbn, bias, cka, cva, ckb, cvb, lams, g_subln, lam_init):
    nb, s_len, w2 = qa.shape
    n_pages = page_table.shape[1]
    nslot = w2 // HEAD_DIM
    rows = s_len * nslot
    gp = 2 if n_pages % 2 == 0 else 1
    n_steps = n_pages // gp
    kern = functools.partial(_attn_s_kernel, n_steps=n_steps, gp=gp, s_len=s_len, nslot=nslot, w2=w2,
                             lam_init=lam_init)
    seq = lambda r: pl.BlockSpec((None, r, w2), lambda b, p, pt: (b, 0, 0))

    def page(g):
        return pl.BlockSpec((PAGE_SIZE * nslot, HEAD_DIM),
                            lambda b, p, pt: (pt[b, jnp.minimum(p, n_steps - 1) * gp + g], 0))

    pages = [page(g) for _ in range(4) for g in range(gp)]
    vec = pl.BlockSpec((1, HEAD_DIM), lambda b, p, pt: (0, 0))
    grid_spec = pltpu.PrefetchScalarGridSpec(
        num_scalar_prefetch=1,
        grid=(nb, n_steps + 1),
        in_specs=[seq(s_len), seq(s_len), seq(8), seq(8), seq(8), seq(8),
                  pl.BlockSpec((None, s_len, gp * PAGE_SIZE), lambda b, p, pt: (b, 0, jnp.minimum(p, n_steps - 1))),
                  pl.BlockSpec((None, s_len, PAGE_SIZE), lambda b, p, pt: (b, 0, n_pages)),
                  *pages, vec, vec, vec, vec,
                  pl.BlockSpec((1, 2 * HEAD_DIM), lambda b, p, pt: (0, 0))],
        out_specs=[seq(s_len), seq(s_len)],
        scratch_shapes=[pltpu.VMEM((rows, w2), BF16), pltpu.VMEM((rows, w2), BF16),
                        pltpu.VMEM((rows, LANES), F32), pltpu.VMEM((rows, LANES), F32), pltpu.VMEM((rows, w2), F32),
                        pltpu.VMEM((rows, LANES), F32), pltpu.VMEM((rows, LANES), F32), pltpu.VMEM((rows, w2), F32)],
    )
    return pl.pallas_call(
        kern,
        grid_spec=grid_spec,
        out_shape=[jax.ShapeDtypeStruct((nb, s_len, w2), F32), jax.ShapeDtypeStruct((nb, s_len, w2), F32)],
        compiler_params=_cparams(("arbitrary", "arbitrary")),
        name="attn_sample",
    )(page_table, qa, qb, kan, van, kbn, vbn, bias, bias, *([cka] * gp), *([cva] * gp), *([ckb] * gp),
      *([cvb] * gp), *lams, g_subln)


def _resid_kernel(a_ref, w_ref, x_ref, g_ref, o_ref, *acc_scr, nk, mr):
    def finish(acc):
        tm = acc.shape[0]
        if mr == 1:
            o_ref[...] = x_ref[...] + g_ref[...] * acc
        else:
            for s in range(tm // mr):
                o_ref[s * mr:(s + 1) * mr, :] = x_ref[s * mr:(s + 1) * mr, :] + g_ref[...] * acc[s * mr:(s + 1) * mr, :]

    part = jnp.dot(a_ref[...], w_ref[...], preferred_element_type=F32)
    if nk == 1:
        finish(part)
    else:
        k = pl.program_id(2)

        @pl.when(k == 0)
        def _():
            acc_scr[0][...] = part

        @pl.when((k > 0) & (k < nk - 1))
        def _():
            acc_scr[0][...] += part

        @pl.when(k == nk - 1)
        def _():
            finish(acc_scr[0][...] + part)


def _resid_mm(a, w, x, mod, gate_blk, tm, tn, tk):
    rows, kd = a.shape
    n = w.shape[1]
    mr = mod.shape[0]
    nk = kd // tk
    kern = functools.partial(_resid_kernel, nk=nk, mr=mr)
    gate_off = gate_blk * (n // tn)
    return pl.pallas_call(
        kern,
        grid=(rows // tm, n // tn, nk),
        in_specs=[pl.BlockSpec((tm, tk), lambda i, j, k: (i, k)),
                  pl.BlockSpec((tk, tn), lambda i, j, k: (k, j)),
                  pl.BlockSpec((tm, tn), lambda i, j, k: (i, j)),
                  pl.BlockSpec((mr, tn), lambda i, j, k: (0, gate_off + j))],
        out_specs=pl.BlockSpec((tm, tn), lambda i, j, k: (i, j)),
        out_shape=jax.ShapeDtypeStruct((rows, n), F32),
        scratch_shapes=[pltpu.VMEM((tm, tn), F32)] if nk > 1 else [],
        compiler_params=_cparams(("arbitrary", "arbitrary", "arbitrary")),
        name="resid_mm",
    )(a, w, x, mod)


def _up_kernel(h_ref, wg_ref, wv_ref, prev_ref, wc_ref, bc_ref, u_ref, tail_ref, carry_scr, *, hp, shift):
    i = pl.program_id(1)

    @pl.when(i == 0)
    def _():
        carry_scr[...] = prev_ref[...]

    h = h_ref[...]
    g = jnp.dot(h, wg_ref[...], preferred_element_type=F32)
    v = jnp.dot(h, wv_ref[...], preferred_element_type=F32)
    tm = g.shape[0]
    gcat = jnp.concatenate([carry_scr[...], g], axis=0)
    wc = wc_ref[...]
    conv = (bc_ref[...]
            + wc[0:1, :] * gcat[hp - 2 * shift:hp - 2 * shift + tm, :]
            + wc[1:2, :] * gcat[hp - shift:hp - shift + tm, :]
            + wc[2:3, :] * g)
    u_ref[...] = (conv * jax.nn.sigmoid(conv) * v).astype(u_ref.dtype)
    tail = g[tm - hp:, :]
    carry_scr[...] = tail
    tail_ref[...] = tail


def _up(h2, w_up16, prev, w_conv, b_conv, tm, hp, shift):
    rows, d = h2.shape
    ff = w_conv.shape[1]
    tn = _pick(ff, (256, 128))
    nj = ff // tn
    kern = functools.partial(_up_kernel, hp=hp, shift=shift)
    return pl.pallas_call(
        kern,
        grid=(nj, rows // tm),
        in_specs=[pl.BlockSpec((tm, d), lambda j, i: (i, 0)),
                  pl.BlockSpec((d, tn), lambda j, i: (0, j)),
                  pl.BlockSpec((d, tn), lambda j, i: (0, nj + j)),
                  pl.BlockSpec((hp, tn), lambda j, i: (0, j)),
                  pl.BlockSpec((CONV_W, tn), lambda j, i: (0, j)),
                  pl.BlockSpec((1, tn), lambda j, i: (0, j))],
        out_specs=[pl.BlockSpec((tm, tn), lambda j, i: (i, j)),
                   pl.BlockSpec((hp, tn), lambda j, i: (0, j))],
        out_shape=[jax.ShapeDtypeStruct((rows, ff), BF16), jax.ShapeDtypeStruct((hp, ff), F32)],
        scratch_shapes=[pltpu.VMEM((hp, tn), F32)],
        compiler_params=_cparams(("arbitrary", "arbitrary")),
        name="ffn_up",
    )(h2, w_up16, w_up16, prev, w_conv, b_conv.reshape(1, ff))


def _rope_tables(pos):
    inv = ROPE_THETA ** (-jnp.arange(0, HEAD_DIM, 2, dtype=F32) / HEAD_DIM)
    ang = pos.astype(F32)[:, None] * inv[None, :]
    cos, sin = jnp.cos(ang), jnp.sin(ang)
    return jnp.concatenate([cos, cos], axis=1), jnp.concatenate([-sin, sin], axis=1)


def _layout(d, in_width):
    w2 = d // 2
    n_idx = (in_width - 6 * w2 - IDX_DIM) // (IDX_DIM + 1)
    qi = n_idx * IDX_DIM
    tn = min(512, w2)
    assert qi % w2 == 0 and w2 % tn == 0 and n_idx <= LANES
    lay = dict(w2=w2, n_idx=n_idx, tn=tn, iq=0, qa=qi, ka=qi + w2, va=qi + 2 * w2, qb=qi + 3 * w2,
               kb=qi + 4 * w2, vb=qi + 5 * w2, sp=qi + 6 * w2, total=qi + 6 * w2 + tn)
    return lay


def _cast_w_in(w, lay):
    w16 = w.astype(BF16)
    return w16, w16[:, w.shape[1] - 3 * lay["w2"]:]


def _layer(x2d, mod, pos, prev, attend, lw, lay, tm_rows, tm_norm, hp, shift):
    g_attn, w_in16, w_o16, g_ffn, w_up16, w_conv, b_conv, w_down16, g_final = lw
    rows, d = x2d.shape
    cosf, sinf = _rope_tables(pos)
    h = _norm_mod(x2d, g_attn, mod, 1, 0, tm_norm)
    p32, p16 = _proj(h, w_in16[0], w_in16[1], cosf, sinf, lay, tm_rows)
    o16 = attend(p32, p16)
    tn = _pick(d, (512, 256, 128))
    x1 = _resid_mm(o16, w_o16, x2d, mod, 2, tm_rows, tn, d)
    h2 = _norm_mod(x1, g_ffn, mod, 4, 3, tm_norm)
    u, tail = _up(h2, w_up16, prev, w_conv, b_conv, tm_rows, hp, shift)
    ff = u.shape[1]
    x2 = _resid_mm(u, w_down16, x1, mod, 5, min(tm_rows, 512), tn, ff)
    y = _final_norm(x2, g_final, tm_norm)
    return y, p32, tail


def kernel(x_prompt, x_sample, cache_k_a, cache_v_a, cache_k_idx, cache_k_b, cache_v_b, state_conv, page_table,
           c_prompt, c_sample, w_ada, b_ada, g_attn, w_in, lam_q1, lam_k1, lam_q2, lam_k2, g_subln, w_o, g_ffn,
           w_up, w_conv, b_conv, w_down, g_final):
    depth = w_ada.shape[0]
    assert depth == 1 and x_prompt.shape[0] == 1
    bsz, t_len, d = x_prompt.shape
    nb, s_len, _ = x_sample.shape
    n_pages = page_table.shape[1]
    past_len = n_pages * PAGE_SIZE
    ff = w_conv.shape[-1]
    lay = _layout(d, w_in.shape[-1])
    w2, n_idx = lay["w2"], lay["n_idx"]
    nha, nhb = w2 // HEAD_DIM, w2 // (2 * HEAD_DIM)
    l = 0
    lam_init = 0.8 - 0.6 * math.exp(-0.3 * l)

    lw = (g_attn[l], _cast_w_in(w_in[l], lay), w_o[l].astype(BF16), g_ffn[l], w_up[l].astype(BF16), w_conv[l], b_conv[l],
          w_down[l].astype(BF16), g_final)
    lams = tuple(a[l].reshape(1, HEAD_DIM) for a in (lam_q1, lam_k1, lam_q2, lam_k2))
    gs = g_subln[l].reshape(1, 2 * HEAD_DIM)

    n_c = nb + 16
    c_all = jnp.concatenate([c_sample, c_prompt, jnp.zeros((n_c - nb - 1, d), F32)], axis=0).astype(BF16)
    mod = _ada(c_all, w_ada[l], b_ada[l])
    mod_s, mod_p = mod[:nb], mod[nb:nb + 1]

    tq = _pick(t_len, (256, 128))
    tk = _pick(t_len, (512, 256, 128))
    k_sel_p = min(TOPK_MAX, t_len // 4)

    def attend_p(p32, p16):
        bias = _score_p(p16, p32, lay, tq, tk, k_sel_p)
        oa = _sattn_p(p16, bias, lay, tq, tk)
        ob = _dattn_p(p16, lams, gs, lay, tq, tk, lam_init)
        return jnp.concatenate([oa, ob], axis=1)

    tm_p = _pick(t_len, (1024, 512, 256, 128))
    y_p, p32_p, tail_p = _layer(x_prompt[0], mod_p, jnp.arange(t_len), jnp.zeros((8, ff), F32), attend_p, lw, lay,
                                tm_p, _pick(t_len, (512, 256, 128)), 8, 1)

    def seg(p32, name, width):
        return p32[:, lay[name]:lay[name] + width]

    outs_p = (
        y_p.reshape(1, t_len, d),
        seg(p32_p, "ka", w2).reshape(1, 1, t_len, nha, HEAD_DIM),
        seg(p32_p, "va", w2).reshape(1, 1, t_len, nha, HEAD_DIM),
        seg(p32_p, "sp", IDX_DIM).reshape(1, 1, t_len, IDX_DIM),
        seg(p32_p, "kb", w2).reshape(1, 1, t_len, nhb, 2, HEAD_DIM),
        seg(p32_p, "vb", w2).reshape(1, 1, t_len, nhb, 2 * HEAD_DIM),
        tail_p[8 - (CONV_W - 1):].reshape(1, 1, CONV_W - 1, ff),
    )

    rows_s = s_len * nb
    xs_tm = jnp.transpose(x_sample, (1, 0, 2)).reshape(rows_s, d)
    pos_s = past_len + jnp.repeat(jnp.arange(s_len), nb)
    prev_s = jnp.transpose(state_conv[l], (1, 0, 2)).reshape((CONV_W - 1) * nb, ff)
    k_sel_s = min(TOPK_MAX, (past_len + s_len) // 4)
    n_phys = cache_k_a.shape[1]

    def to_bm(a):
        return jnp.transpose(a.reshape(s_len, nb, a.shape[1]), (1, 0, 2))

    def pad8(a):
        return jnp.pad(a, ((0, 0), (0, 8 - s_len), (0, 0)))

    bm_cache = {}
    flat = lambda c: c.reshape(n_phys * PAGE_SIZE * nha, HEAD_DIM)

    def attend_s(p32, p16):
        bm = to_bm(p32)
        bm_cache["bm"] = bm
        sl = lambda name, width: bm[:, :, lay[name]:lay[name] + width]
        iq = sl("iq", n_idx * IDX_DIM).reshape(nb, s_len * n_idx, IDX_DIM)
        iw = bm[:, :, lay["sp"] + IDX_DIM:lay["sp"] + IDX_DIM + n_idx].reshape(nb, s_len * n_idx, 1)
        scores = _score_s(page_table, iq, iw, pad8(sl("sp", IDX_DIM)),
                          cache_k_idx[l].reshape(n_phys, PAGE_SIZE, IDX_DIM), past_len, s_len)
        l_pad = scores.shape[2]
        bias = _thresh(scores.reshape(nb * s_len, l_pad), k_sel_s).reshape(nb, s_len, l_pad)
        oa, ob = _attn_s(page_table, sl("qa", w2), sl("qb", w2), pad8(sl("ka", w2)), pad8(sl("va", w2)),
                         pad8(sl("kb", w2)), pad8(sl("vb", w2)), bias,
                         flat(cache_k_a[l]), flat(cache_v_a[l]), flat(cache_k_b[l]),
                         flat(jnp.swapaxes(cache_v_b[l].reshape(n_phys, PAGE_SIZE, nhb, 2, HEAD_DIM), 2, 3)),
                         lams, gs, lam_init)
        o = jnp.concatenate([oa, ob], axis=2)
        return jnp.transpose(o, (1, 0, 2)).reshape(rows_s, d).astype(BF16)

    y_s, _, tail_s = _layer(xs_tm, mod_s, pos_s, prev_s, attend_s, lw, lay, rows_s, nb, (CONV_W - 1) * nb, nb)
    bm = bm_cache["bm"]
    sb = lambda name, width: bm[:, :, lay[name]:lay[name] + width]
    outs_s = (
        jnp.transpose(y_s.reshape(s_len, nb, d), (1, 0, 2)),
        sb("ka", w2).reshape(1, nb, s_len, nha, HEAD_DIM),
        sb("va", w2).reshape(1, nb, s_len, nha, HEAD_DIM),
        sb("sp", IDX_DIM).reshape(1, nb, s_len, IDX_DIM),
        sb("kb", w2).reshape(1, nb, s_len, nhb, 2, HEAD_DIM),
        sb("vb", w2).reshape(1, nb, s_len, nhb, 2 * HEAD_DIM),
        jnp.transpose(tail_s.reshape(CONV_W - 1, nb, ff), (1, 0, 2)).reshape(1, nb, CONV_W - 1, ff),
    )
    return (outs_p[0], outs_s[0]) + outs_p[1:] + outs_s[1:]
```

```python
import functools
import math

import jax
import jax.numpy as jnp
from jax import lax
from jax.experimental import pallas as pl
from jax.experimental.pallas import tpu as pltpu

F32 = jnp.float32
BF16 = jnp.bfloat16

HEAD_DIM = 128
IDX_DIM = 128
PAGE_SIZE = 128
TOPK_MAX = 256
CONV_W = 3
ROPE_THETA = 10000.0
EPS = 1e-6
HEAD_SCALE = HEAD_DIM ** -0.5
SCORE_SCALE = HEAD_SCALE * math.log2(math.e)
IDX_SCALE = IDX_DIM ** -0.5
LANES = 128
NEG = -1e30
INT_MIN = -(2 ** 31)
KEY_NEG_INF = INT_MIN + 0x7FFFFF
VMEM_LIMIT = 56 * 1024 * 1024


def _cparams(sem):
    return pltpu.CompilerParams(dimension_semantics=sem, vmem_limit_bytes=VMEM_LIMIT)


def _pick(n, cands):
    for c in cands:
        if n % c == 0:
            return c
    return n


def _ada_kernel(c_ref, w_ref, b_ref, o_ref):
    acc = jnp.dot(c_ref[...], w_ref[...].astype(BF16), preferred_element_type=F32)
    o_ref[...] = acc + b_ref[...]


def _ada(c_all, w_ada, b_ada):
    r, d = c_all.shape
    n = w_ada.shape[1]
    tn = _pick(n, (512, 256, 128))
    return pl.pallas_call(
        _ada_kernel,
        grid=(n // tn,),
        in_specs=[pl.BlockSpec((r, d), lambda j: (0, 0)),
                  pl.BlockSpec((d, tn), lambda j: (0, j)),
                  pl.BlockSpec((1, tn), lambda j: (0, j))],
        out_specs=pl.BlockSpec((r, tn), lambda j: (0, j)),
        out_shape=jax.ShapeDtypeStruct((r, n), F32),
        compiler_params=_cparams(("arbitrary",)),
        name="ada",
    )(c_all, w_ada, b_ada.reshape(1, n))


def _norm_mod_kernel(x_ref, g_ref, sc_ref, sh_ref, o_ref):
    x = x_ref[...]
    y = x * lax.rsqrt(jnp.mean(x * x, axis=-1, keepdims=True) + EPS) * g_ref[...]
    o_ref[...] = (y * (1.0 + sc_ref[...]) + sh_ref[...]).astype(o_ref.dtype)


def _norm_mod(x, g, mod, sc_blk, sh_blk, tm):
    rows, d = x.shape
    mr = mod.shape[0]
    assert mr == 1 or mr == tm
    return pl.pallas_call(
        _norm_mod_kernel,
        grid=(rows // tm,),
        in_specs=[pl.BlockSpec((tm, d), lambda i: (i, 0)),
                  pl.BlockSpec((1, d), lambda i: (0, 0)),
                  pl.BlockSpec((mr, d), lambda i: (0, sc_blk)),
                  pl.BlockSpec((mr, d), lambda i: (0, sh_blk))],
        out_specs=pl.BlockSpec((tm, d), lambda i: (i, 0)),
        out_shape=jax.ShapeDtypeStruct((rows, d), BF16),
        compiler_params=_cparams(("arbitrary",)),
        name="norm_mod",
    )(x, g.reshape(1, d), mod, mod)


def _final_norm_kernel(x_ref, g_ref, o_ref):
    x = x_ref[...]
    o_ref[...] = x * lax.rsqrt(jnp.mean(x * x, axis=-1, keepdims=True) + EPS) * g_ref[...]


def _final_norm(x, g, tm):
    rows, d = x.shape
    return pl.pallas_call(
        _final_norm_kernel,
        grid=(rows // tm,),
        in_specs=[pl.BlockSpec((tm, d), lambda i: (i, 0)),
                  pl.BlockSpec((1, d), lambda i: (0, 0))],
        out_specs=pl.BlockSpec((tm, d), lambda i: (i, 0)),
        out_shape=jax.ShapeDtypeStruct((rows, d), F32),
        compiler_params=_cparams(("arbitrary",)),
        name="final_norm",
    )(x, g.reshape(1, d))


def _rope_cols(a, cos, sin):
    outs = []
    for c in range(a.shape[1] // HEAD_DIM):
        xh = a[:, c * HEAD_DIM:(c + 1) * HEAD_DIM]
        outs.append(xh * cos + pltpu.roll(xh, HEAD_DIM // 2, 1) * sin)
    return outs[0] if len(outs) == 1 else jnp.concatenate(outs, axis=1)


def _proj_kernel(h_ref, wn_ref, wt_ref, cos_ref, sin_ref, o32_ref, o16_ref, acc_scr, *, n_reg, v_lo, v_hi, iw_scale):
    j = pl.program_id(1)
    jt = j - n_reg - 1

    @pl.when(j <= n_reg)
    def _():
        acc_scr[...] = jnp.dot(h_ref[...], wn_ref[...], preferred_element_type=F32)

    @pl.when(j > n_reg)
    def _():
        acc_scr[...] = jnp.dot(h_ref[...], wt_ref[...], preferred_element_type=F32)

    acc = acc_scr[...]
    is_plain = ((j >= v_lo) & (j < v_hi)) | ((jt >= v_lo) & (jt < v_hi))
    is_special = j == n_reg

    def emit(v):
        o32_ref[...] = v
        o16_ref[...] = v.astype(BF16)

    @pl.when(is_plain)
    def _():
        emit(acc)

    @pl.when(is_special)
    def _():
        parts = [_rope_cols(acc[:, :IDX_DIM], cos_ref[...], sin_ref[...]), acc[:, IDX_DIM:2 * IDX_DIM] * iw_scale]
        if acc.shape[1] > 2 * IDX_DIM:
            parts.append(acc[:, 2 * IDX_DIM:])
        emit(jnp.concatenate(parts, axis=1))

    @pl.when(jnp.logical_not(is_plain | is_special))
    def _():
        emit(_rope_cols(acc, cos_ref[...], sin_ref[...]))


def _proj(h, w_nat, w_tail, cosf, sinf, lay, tm):
    rows, d = h.shape
    tn, w2 = lay["tn"], lay["w2"]
    n_a, n_b = 3 * w2 // tn, lay["qa"] // tn
    n_reg = n_a + n_b
    n = lay["total"]
    assert (n_reg + 1) * tn <= w_nat.shape[1] and w_tail.shape[1] == n_a * tn

    def out_blk(j):
        return jnp.where(j < n_a, j + n_b, jnp.where(j < n_reg, j - n_a, jnp.where(j == n_reg, n_reg + n_a, j - 1)))

    kern = functools.partial(_proj_kernel, n_reg=n_reg, v_lo=2 * w2 // tn, v_hi=n_a, iw_scale=lay["n_idx"] ** -0.5)
    return pl.pallas_call(
        kern,
        grid=(rows // tm, n // tn),
        in_specs=[pl.BlockSpec((tm, d), lambda i, j: (i, 0)),
                  pl.BlockSpec((d, tn), lambda i, j: (0, jnp.minimum(j, n_reg))),
                  pl.BlockSpec((d, tn), lambda i, j: (0, jnp.maximum(j - n_reg - 1, 0))),
                  pl.BlockSpec((tm, HEAD_DIM), lambda i, j: (i, 0)),
                  pl.BlockSpec((tm, HEAD_DIM), lambda i, j: (i, 0))],
        out_specs=[pl.BlockSpec((tm, tn), lambda i, j: (i, out_blk(j))),
                   pl.BlockSpec((tm, tn), lambda i, j: (i, out_blk(j)))],
        out_shape=[jax.ShapeDtypeStruct((rows, n), F32), jax.ShapeDtypeStruct((rows, n), BF16)],
        scratch_shapes=[pltpu.VMEM((tm, tn), F32)],
        compiler_params=_cparams(("arbitrary", "arbitrary")),
        name="proj",
    )(h, w_nat, w_tail, cosf, sinf)


def _float_key(x):
    bits = pltpu.bitcast(x, jnp.int32)
    return bits ^ ((bits >> 31) & 0x7FFFFFFF)


def _topk_bias(key_ref, thr_ref, o_ref, *, rows, nch, total_ch, k, rg, unroll):
    for g in range(rows // rg):
        r0 = g * rg

        def bit_body(b, t, r0=r0):
            cand = t + lax.shift_left(jnp.int32(1), jnp.int32(31) - b)

            def ch_body(c, cnt):
                for u in range(unroll):
                    sl = pl.ds(pl.multiple_of((c * unroll + u) * LANES, LANES), LANES)
                    cnt = cnt + jnp.where(key_ref[r0:r0 + rg, sl] >= cand, 1.0, 0.0)
                return cnt

            cnt = lax.fori_loop(0, nch // unroll, ch_body, jnp.zeros((rg, LANES), F32))
            tot = jnp.sum(cnt, axis=1, keepdims=True)
            return jnp.where(tot >= float(k), cand, t)

        t = lax.fori_loop(0, 32, bit_body, jnp.full((rg, LANES), INT_MIN, jnp.int32))
        thr_ref[r0:r0 + rg, :] = jnp.maximum(t, KEY_NEG_INF + 1)

    thr = thr_ref[...]

    def out_body(c, carry):
        sl = pl.ds(pl.multiple_of(c * LANES, LANES), LANES)
        o_ref[:, sl] = jnp.where(key_ref[:, sl] >= thr, 0.0, NEG).astype(o_ref.dtype)
        return carry

    def fill_body(c, carry):
        sl = pl.ds(pl.multiple_of(c * LANES, LANES), LANES)
        o_ref[:, sl] = jnp.full((rows, LANES), NEG, o_ref.dtype)
        return carry

    lax.fori_loop(0, nch, out_body, 0)
    lax.fori_loop(nch, total_ch, fill_body, 0)


def _score_p_kernel(iq_ref, ik_ref, iw_ref, o_ref, key_scr, wb_scr, thr_scr, *, tq, tk, n_idx, k_sel, nj, t_len):
    i = pl.program_id(0)
    j = pl.program_id(1)
    jmax = ((i + 1) * tq - 1) // tk

    @pl.when(j == 0)
    def _():
        w = iw_ref[...] * IDX_SCALE
        for h in range(n_idx):
            wb_scr[h] = jnp.broadcast_to(w[:, h:h + 1], (tq, LANES))

    @pl.when(j <= jmax)
    def _():
        kb = ik_ref[...]
        acc = jnp.zeros((tq, tk), F32)
        for h in range(n_idx):
            s = lax.dot_general(iq_ref[:, h * IDX_DIM:(h + 1) * IDX_DIM], kb, (((1,), (1,)), ((), ())),
                                preferred_element_type=F32)
            acc = acc + jnp.maximum(s, 0.0) * pltpu.repeat(wb_scr[h], tk // LANES, axis=1)
        rows = i * tq + lax.broadcasted_iota(jnp.int32, (tq, tk), 0)
        cols = j * tk + lax.broadcasted_iota(jnp.int32, (tq, tk), 1)
        acc = jnp.where(cols <= rows, acc, -jnp.inf)
        key_scr[:, pl.ds(pl.multiple_of(j * tk, tk), tk)] = _float_key(acc)

    @pl.when(j == nj - 1)
    def _():
        _topk_bias(key_scr, thr_scr, o_ref, rows=tq, nch=(jmax + 1) * (tk // LANES), total_ch=t_len // LANES,
                   k=k_sel, rg=min(tq, 128), unroll=tk // LANES)


def _score_p(p16, p32, lay, tq, tk, k_sel):
    t_len = p16.shape[0]
    n_idx = lay["n_idx"]
    qi = n_idx * IDX_DIM
    ni, nj = t_len // tq, t_len // tk
    sp_blk = lay["sp"] // IDX_DIM
    kern = functools.partial(_score_p_kernel, tq=tq, tk=tk, n_idx=n_idx, k_sel=k_sel, nj=nj, t_len=t_len)
    return pl.pallas_call(
        kern,
        grid=(ni, nj),
        in_specs=[pl.BlockSpec((tq, qi), lambda i, j: (i, 0)),
                  pl.BlockSpec((tk, IDX_DIM), lambda i, j: (jnp.minimum(j, ((i + 1) * tq - 1) // tk), sp_blk)),
                  pl.BlockSpec((tq, IDX_DIM), lambda i, j: (i, sp_blk + 1))],
        out_specs=pl.BlockSpec((tq, t_len), lambda i, j: (i, 0)),
        out_shape=jax.ShapeDtypeStruct((t_len, t_len), BF16),
        scratch_shapes=[pltpu.VMEM((tq, t_len), jnp.int32),
                        pltpu.VMEM((n_idx, tq, LANES), F32),
                        pltpu.VMEM((tq, LANES), jnp.int32)],
        compiler_params=_cparams(("arbitrary", "arbitrary")),
        name="score_prompt",
    )(p16, p16, p32)


def _thresh_kernel(s_ref, o_ref, key_scr, thr_scr, *, rows, nch, k_sel):
    key_scr[...] = _float_key(s_ref[...])
    _topk_bias(key_scr, thr_scr, o_ref, rows=rows, nch=nch, total_ch=nch, k=k_sel, rg=min(rows, 64), unroll=nch)


def _thresh(s, k_sel):
    rows, l = s.shape
    kern = functools.partial(_thresh_kernel, rows=rows, nch=l // LANES, k_sel=k_sel)
    return pl.pallas_call(
        kern,
        grid=(1,),
        in_specs=[pl.BlockSpec((rows, l), lambda i: (0, 0))],
        out_specs=pl.BlockSpec((rows, l), lambda i: (0, 0)),
        out_shape=jax.ShapeDtypeStruct((rows, l), F32),
        scratch_shapes=[pltpu.VMEM((rows, l), jnp.int32), pltpu.VMEM((rows, LANES), jnp.int32)],
        compiler_params=_cparams(("arbitrary",)),
        name="thresh_sample",
    )(s)


def _online_step(s, v, m_scr, l_scr, hd, acc_ref, c0, c1):
    tk = s.shape[1]
    m_prev = m_scr[hd]
    l_prev = l_scr[hd]
    m_new = jnp.maximum(m_prev, jnp.max(s, axis=1, keepdims=True))
    alpha = jnp.exp2(m_prev - m_new)
    p = jnp.exp2(s - pltpu.repeat(m_new, tk // LANES, axis=1))
    l_scr[hd] = alpha * l_prev + jnp.sum(p, axis=1, keepdims=True)
    m_scr[hd] = m_new
    pv = jnp.dot(p.astype(BF16), v, preferred_element_type=F32)
    arep = alpha if (c1 - c0) == LANES else pltpu.repeat(alpha, (c1 - c0) // LANES, axis=1)
    acc_ref[:, c0:c1] = acc_ref[:, c0:c1] * arep + pv


def _qk(q, k):
    return lax.dot_general(q, k, (((1,), (1,)), ((), ())), preferred_element_type=F32)


def _sattn_p_kernel(q_ref, k_ref, v_ref, b_ref, o_ref, m_scr, l_scr, acc_scr, *, tq, tk, nh, nj):
    i = pl.program_id(0)
    j = pl.program_id(1)
    jmax = ((i + 1) * tq - 1) // tk

    @pl.when(j == 0)
    def _():
        m_scr[...] = jnp.full(m_scr.shape, NEG, F32)
        l_scr[...] = jnp.zeros(l_scr.shape, F32)
        acc_scr[...] = jnp.zeros(acc_scr.shape, F32)

    @pl.when(j <= jmax)
    def _():
        bias = b_ref[...].astype(F32)
        for h in range(nh):
            c0, c1 = h * HEAD_DIM, (h + 1) * HEAD_DIM
            s = _qk(q_ref[:, c0:c1], k_ref[:, c0:c1]) * SCORE_SCALE + bias
            _online_step(s, v_ref[:, c0:c1], m_scr, l_scr, h, acc_scr, c0, c1)

    @pl.when(j == nj - 1)
    def _():
        for h in range(nh):
            c0, c1 = h * HEAD_DIM, (h + 1) * HEAD_DIM
            o_ref[:, c0:c1] = (acc_scr[:, c0:c1] / l_scr[h]).astype(o_ref.dtype)


def _sattn_p(p16, bias, lay, tq, tk):
    t_len = p16.shape[0]
    w2 = lay["w2"]
    nh = w2 // HEAD_DIM
    ni, nj = t_len // tq, t_len // tk
    jm = lambda i, j: jnp.minimum(j, ((i + 1) * tq - 1) // tk)
    kern = functools.partial(_sattn_p_kernel, tq=tq, tk=tk, nh=nh, nj=nj)
    return pl.pallas_call(
        kern,
        grid=(ni, nj),
        in_specs=[pl.BlockSpec((tq, w2), lambda i, j: (i, lay["qa"] // w2)),
                  pl.BlockSpec((tk, w2), lambda i, j: (jm(i, j), lay["ka"] // w2)),
                  pl.BlockSpec((tk, w2), lambda i, j: (jm(i, j), lay["va"] // w2)),
                  pl.BlockSpec((tq, tk), lambda i, j: (i, jm(i, j)))],
        out_specs=pl.BlockSpec((tq, w2), lambda i, j: (i, 0)),
        out_shape=jax.ShapeDtypeStruct((t_len, w2), BF16),
        scratch_shapes=[pltpu.VMEM((nh, tq, LANES), F32), pltpu.VMEM((nh, tq, LANES), F32),
                        pltpu.VMEM((tq, w2), F32)],
        compiler_params=_cparams(("arbitrary", "arbitrary")),
        name="sparse_attn_prompt",
    )(p16, p16, p16, bias)


def _lam_value(lq1, lk1, lq2, lk2, lam_init):
    a = jnp.sum(lq1[...] * lk1[...], axis=1, keepdims=True)
    b = jnp.sum(lq2[...] * lk2[...], axis=1, keepdims=True)
    return jnp.exp(a) - jnp.exp(b) + lam_init


def _subln(o, g, lam_init):
    return o * lax.rsqrt(jnp.mean(o * o, axis=1, keepdims=True) + EPS) * g * (1.0 - lam_init)


def _dattn_p_kernel(q_ref, k_ref, v_ref, lq1, lk1, lq2, lk2, g_ref, o_ref, m_scr, l_scr, acc0_scr, acc1_scr,
                    *, tq, tk, nhb, nj, lam_init):
    i = pl.program_id(0)
    j = pl.program_id(1)
    jmax = ((i + 1) * tq - 1) // tk
    dv = 2 * HEAD_DIM

    @pl.when(j == 0)
    def _():
        m_scr[...] = jnp.full(m_scr.shape, NEG, F32)
        l_scr[...] = jnp.zeros(l_scr.shape, F32)
        acc0_scr[...] = jnp.zeros(acc0_scr.shape, F32)
        acc1_scr[...] = jnp.zeros(acc1_scr.shape, F32)

    @pl.when(j <= jmax)
    def _():
        rows = i * tq + lax.broadcasted_iota(jnp.int32, (tq, tk), 0)
        cols = j * tk + lax.broadcasted_iota(jnp.int32, (tq, tk), 1)
        bias = jnp.where(cols <= rows, 0.0, NEG)
        for h in range(nhb):
            v = v_ref[:, h * dv:(h + 1) * dv]
            for c, acc in ((0, acc0_scr), (1, acc1_scr)):
                hd = 2 * h + c
                c0, c1 = hd * HEAD_DIM, (hd + 1) * HEAD_DIM
                s = _qk(q_ref[:, c0:c1], k_ref[:, c0:c1]) * SCORE_SCALE + bias
                _online_step(s, v, m_scr, l_scr, hd, acc, h * dv, (h + 1) * dv)

    @pl.when(j == nj - 1)
    def _():
        lam = _lam_value(lq1, lk1, lq2, lk2, lam_init)
        g = g_ref[...]
        for h in range(nhb):
            c0, c1 = h * dv, (h + 1) * dv
            o0 = acc0_scr[:, c0:c1] / pltpu.repeat(l_scr[2 * h], 2, axis=1)
            o1 = acc1_scr[:, c0:c1] / pltpu.repeat(l_scr[2 * h + 1], 2, axis=1)
            o_ref[:, c0:c1] = _subln(o0 - lam * o1, g, lam_init).astype(o_ref.dtype)


def _dattn_p(p16, lams, g_subln, lay, tq, tk, lam_init):
    t_len = p16.shape[0]
    w2 = lay["w2"]
    nhb = w2 // (2 * HEAD_DIM)
    ni, nj = t_len // tq, t_len // tk
    jm = lambda i, j: jnp.minimum(j, ((i + 1) * tq - 1) // tk)
    kern = functools.partial(_dattn_p_kernel, tq=tq, tk=tk, nhb=nhb, nj=nj, lam_init=lam_init)
    vec = pl.BlockSpec((1, HEAD_DIM), lambda i, j: (0, 0))
    return pl.pallas_call(
        kern,
        grid=(ni, nj),
        in_specs=[pl.BlockSpec((tq, w2), lambda i, j: (i, lay["qb"] // w2)),
                  pl.BlockSpec((tk, w2), lambda i, j: (jm(i, j), lay["kb"] // w2)),
                  pl.BlockSpec((tk, w2), lambda i, j: (jm(i, j), lay["vb"] // w2)),
                  vec, vec, vec, vec,
                  pl.BlockSpec((1, 2 * HEAD_DIM), lambda i, j: (0, 0))],
        out_specs=pl.BlockSpec((tq, w2), lambda i, j: (i, 0)),
        out_shape=jax.ShapeDtypeStruct((t_len, w2), BF16),
        scratch_shapes=[pltpu.VMEM((2 * nhb, tq, LANES), F32), pltpu.VMEM((2 * nhb, tq, LANES), F32),
                        pltpu.VMEM((tq, w2), F32), pltpu.VMEM((tq, w2), F32)],
        compiler_params=_cparams(("arbitrary", "arbitrary")),
        name="diff_attn_prompt",
    )(p16, p16, p16, *lams, g_subln)


def _score_s_kernel(pt_ref, iq_ref, iw_ref, iknew_ref, *rest, n_pages, s_len, n_idx, past_len):
    page_refs = rest[:n_pages]
    o_ref = rest[n_pages]
    q = iq_ref[...].astype(BF16)
    w = iw_ref[...] * IDX_SCALE
    knew = jnp.concatenate([iknew_ref[...], jnp.zeros((PAGE_SIZE - 8, IDX_DIM), F32)], axis=0)
    blocks = [r[...] for r in page_refs] + [knew]
    for p, kb in enumerate(blocks):
        s = _qk(q, kb.astype(BF16))
        s = jnp.maximum(s, 0.0) * w
        sc = jnp.sum(s.reshape(s_len, n_idx, PAGE_SIZE), axis=1)
        cols = p * PAGE_SIZE + lax.broadcasted_iota(jnp.int32, (s_len, PAGE_SIZE), 1)
        qpos = past_len + lax.broadcasted_iota(jnp.int32, (s_len, PAGE_SIZE), 0)
        o_ref[:, p * PAGE_SIZE:(p + 1) * PAGE_SIZE] = jnp.where(cols <= qpos, sc, -jnp.inf)


def _score_s(page_table, iq_s, iw_s, iknew8, ck_idx, past_len, s_len):
    nb, rows, _ = iq_s.shape
    n_pages = page_table.shape[1]
    n_idx = rows // s_len
    l_pad = (n_pages + 1) * PAGE_SIZE
    kern = functools.partial(_score_s_kernel, n_pages=n_pages, s_len=s_len, n_idx=n_idx, past_len=past_len)
    page_specs = [pl.BlockSpec((None, PAGE_SIZE, IDX_DIM), functools.partial(lambda b, pt, p: (pt[b, p], 0, 0), p=p))
                  for p in range(n_pages)]
    grid_spec = pltpu.PrefetchScalarGridSpec(
        num_scalar_prefetch=1,
        grid=(nb,),
        in_specs=[pl.BlockSpec((None, rows, IDX_DIM), lambda b, pt: (b, 0, 0)),
                  pl.BlockSpec((None, rows, 1), lambda b, pt: (b, 0, 0)),
                  pl.BlockSpec((None, 8, IDX_DIM), lambda b, pt: (b, 0, 0))] + page_specs,
        out_specs=pl.BlockSpec((None, s_len, l_pad), lambda b, pt: (b, 0, 0)),
    )
    return pl.pallas_call(
        kern,
        grid_spec=grid_spec,
        out_shape=jax.ShapeDtypeStruct((nb, s_len, l_pad), F32),
        compiler_params=_cparams(("arbitrary",)),
        name="score_sample",
    )(page_table, iq_s, iw_s, iknew8, *([ck_idx] * n_pages))


def _blockdiag_q(q, n_slots, width):
    s_len, w = q.shape
    rep = jnp.broadcast_to(q[:, None, :], (s_len, n_slots, w)).reshape(s_len * n_slots, w)
    return jnp.where(_diag_mask(s_len, n_slots, w, width), rep, 0.0)


def _diag_mask(s_len, n_slots, w, width):
    r = lax.broadcasted_iota(jnp.int32, (s_len * n_slots, w), 0) % n_slots
    c = lax.broadcasted_iota(jnp.int32, (s_len * n_slots, w), 1) // width
    return r == c


def _page_step(qbd_scr, k, v, bias, m_scr, l_scr, acc_scr):
    s = _qk(qbd_scr[...], k) * SCORE_SCALE + bias
    m_prev = m_scr[...]
    m_new = jnp.maximum(m_prev, jnp.max(s, axis=1, keepdims=True))
    alpha = jnp.exp2(m_prev - m_new)
    nrep = s.shape[1] // LANES
    p = jnp.exp2(s - (m_new if nrep == 1 else pltpu.repeat(m_new, nrep, axis=1)))
    l_scr[...] = alpha * l_scr[...] + jnp.sum(p, axis=1, keepdims=True)
    m_scr[...] = m_new
    pv = jnp.dot(p.astype(BF16), v, preferred_element_type=F32)
    acc_scr[...] = acc_scr[...] * pltpu.repeat(alpha, acc_scr.shape[1] // LANES, axis=1) + pv


def _attn_s_kernel(pt_ref, qa_ref, qb_ref, kan_ref, van_ref, kbn_ref, vbn_ref, bias_ref, biasn_ref, *rest,
                   n_steps, gp, s_len, nslot, w2, lam_init):
    cka_refs, cva_refs, ckb_refs, cvb_refs = (rest[c * gp:(c + 1) * gp] for c in range(4))
    (lq1, lk1, lq2, lk2, g_ref, oa_ref, ob_ref,
     qa_scr, qb_scr, ma_scr, la_scr, acca_scr, mb_scr, lb_scr, accb_scr) = rest[4 * gp:]
    p = pl.program_id(1)
    rows = s_len * nslot

    @pl.when(p == 0)
    def _():
        qa_scr[...] = _blockdiag_q(qa_ref[...], nslot, HEAD_DIM).astype(BF16)
        qb_scr[...] = _blockdiag_q(qb_ref[...], nslot, HEAD_DIM).astype(BF16)
        for m_scr, l_scr, acc_scr in ((ma_scr, la_scr, acca_scr), (mb_scr, lb_scr, accb_scr)):
            m_scr[...] = jnp.full(m_scr.shape, NEG, F32)
            l_scr[...] = jnp.zeros(l_scr.shape, F32)
            acc_scr[...] = jnp.zeros(acc_scr.shape, F32)

    def bias_rows(b4):
        return jnp.broadcast_to(b4[:, None, :], (s_len, nslot, b4.shape[1])).reshape(rows, b4.shape[1])

    def load_page(ref, starts):
        return jnp.concatenate([ref[pl.ds(s, PAGE_SIZE, stride=nslot), :].astype(BF16) for s in starts], axis=1)

    def load_pages(refs, starts):
        pages = [load_page(r, starts) for r in refs]
        return pages[0] if len(pages) == 1 else jnp.concatenate(pages, axis=0)

    slots = tuple(range(nslot))
    vb_slots = tuple((s % 2) * (nslot // 2) + s // 2 for s in slots)

    @pl.when(p < n_steps)
    def _():
        _page_step(qa_scr, load_pages(cka_refs, slots), load_pages(cva_refs, slots), bias_rows(bias_ref[...]),
                   ma_scr, la_scr, acca_scr)
        _page_step(qb_scr, load_pages(ckb_refs, slots), load_pages(cvb_refs, vb_slots), 0.0,
                   mb_scr, lb_scr, accb_scr)

    @pl.when(p == n_steps)
    def _():
        pad = lambda r: jnp.concatenate([r[...], jnp.zeros((PAGE_SIZE - 8, w2), F32)], axis=0).astype(BF16)
        _page_step(qa_scr, pad(kan_ref), pad(van_ref), bias_rows(biasn_ref[...]), ma_scr, la_scr, acca_scr)
        kcol = lax.broadcasted_iota(jnp.int32, (s_len, PAGE_SIZE), 1)
        qrow = lax.broadcasted_iota(jnp.int32, (s_len, PAGE_SIZE), 0)
        causal = jnp.where(kcol <= qrow, 0.0, NEG)
        _page_step(qb_scr, pad(kbn_ref), pad(vbn_ref), bias_rows(causal), mb_scr, lb_scr, accb_scr)

        oa = jnp.where(_diag_mask(s_len, nslot, w2, HEAD_DIM), acca_scr[...] / la_scr[...][:, :1], 0.0)
        oa_ref[...] = jnp.sum(oa.reshape(s_len, nslot, w2), axis=1)

        nb = accb_scr[...] / lb_scr[...][:, :1]
        r = lax.broadcasted_iota(jnp.int32, (rows, w2), 0) % nslot
        c = lax.broadcasted_iota(jnp.int32, (rows, w2), 1) // (2 * HEAD_DIM)
        own = (r // 2) == c
        o0 = jnp.sum(jnp.where(own & (r % 2 == 0), nb, 0.0).reshape(s_len, nslot, w2), axis=1)
        o1 = jnp.sum(jnp.where(own & (r % 2 == 1), nb, 0.0).reshape(s_len, nslot, w2), axis=1)
        ob = o0 - _lam_value(lq1, lk1, lq2, lk2, lam_init) * o1
        g = g_ref[...]
        dv = 2 * HEAD_DIM
        for h in range(w2 // dv):
            ob_ref[:, h * dv:(h + 1) * dv] = _subln(ob[:, h * dv:(h + 1) * dv], g, lam_init)


def _attn_s(page_table, qa, qb, kan, van, kbn, vbn, bias, cka, cva, ckb, cvb, lams, g_subln, lam_init):
    nb, s_len, w2 = qa.shape
    n_pages = page_table.shape[1]
    nslot = w2 // HEAD_DIM
    rows = s_len * nslot
    gp = _pick(n_pages, (4, 2, 1))
    n_steps = n_pages // gp
    kern = functools.partial(_attn_s_kernel, n_steps=n_steps, gp=gp, s_len=s_len, nslot=nslot, w2=w2,
                             lam_init=lam_init)
    seq = lambda r: pl.BlockSpec((None, r, w2), lambda b, p, pt: (b, 0, 0))

    def page(g):
        return pl.BlockSpec((PAGE_SIZE * nslot, HEAD_DIM),
                            lambda b, p, pt: (pt[b, jnp.minimum(p, n_steps - 1) * gp + g], 0))

    pages = [page(g) for _ in range(4) for g in range(gp)]
    vec = pl.BlockSpec((1, HEAD_DIM), lambda b, p, pt: (0, 0))
    grid_spec = pltpu.PrefetchScalarGridSpec(
        num_scalar_prefetch=1,
        grid=(nb, n_steps + 1),
        in_specs=[seq(s_len), seq(s_len), seq(8), seq(8), seq(8), seq(8),
                  pl.BlockSpec((None, s_len, gp * PAGE_SIZE), lambda b, p, pt: (b, 0, jnp.minimum(p, n_steps - 1))),
                  pl.BlockSpec((None, s_len, PAGE_SIZE), lambda b, p, pt: (b, 0, n_pages)),
                  *pages, vec, vec, vec, vec,
                  pl.BlockSpec((1, 2 * HEAD_DIM), lambda b, p, pt: (0, 0))],
        out_specs=[seq(s_len), seq(s_len)],
        scratch_shapes=[pltpu.VMEM((rows, w2), BF16), pltpu.VMEM((rows, w2), BF16),
                        pltpu.VMEM((rows, LANES), F32), pltpu.VMEM((rows, LANES), F32), pltpu.VMEM((rows, w2), F32),
                        pltpu.VMEM((rows, LANES), F32), pltpu.VMEM((rows, LANES), F32), pltpu.VMEM((rows, w2), F32)],
    )
    return pl.pallas_call(
        kern,
        grid_spec=grid_spec,
        out_shape=[jax.ShapeDtypeStruct((nb, s_len, w2), F32), jax.ShapeDtypeStruct((nb, s_len, w2), F32)],
        compiler_params=_cparams(("arbitrary", "arbitrary")),
        name="attn_sample",
    )(page_table, qa, qb, kan, van, kbn, vbn, bias, bias, *([cka] * gp), *([cva] * gp), *([ckb] * gp),
      *([cvb] * gp), *lams, g_subln)


def _resid_kernel(a_ref, w_ref, x_ref, g_ref, o_ref, *acc_scr, nk, mr):
    def finish(acc):
        tm = acc.shape[0]
        if mr == 1:
            o_ref[...] = x_ref[...] + g_ref[...] * acc
        else:
            for s in range(tm // mr):
                o_ref[s * mr:(s + 1) * mr, :] = x_ref[s * mr:(s + 1) * mr, :] + g_ref[...] * acc[s * mr:(s + 1) * mr, :]

    part = jnp.dot(a_ref[...], w_ref[...], preferred_element_type=F32)
    if nk == 1:
        finish(part)
    else:
        k = pl.program_id(2)

        @pl.when(k == 0)
        def _():
            acc_scr[0][...] = part

        @pl.when((k > 0) & (k < nk - 1))
        def _():
            acc_scr[0][...] += part

        @pl.when(k == nk - 1)
        def _():
            finish(acc_scr[0][...] + part)


def _resid_mm(a, w, x, mod, gate_blk, tm, tn, tk):
    rows, kd = a.shape
    n = w.shape[1]
    mr = mod.shape[0]
    nk = kd // tk
    kern = functools.partial(_resid_kernel, nk=nk, mr=mr)
    gate_off = gate_blk * (n // tn)
    return pl.pallas_call(
        kern,
        grid=(rows // tm, n // tn, nk),
        in_specs=[pl.BlockSpec((tm, tk), lambda i, j, k: (i, k)),
                  pl.BlockSpec((tk, tn), lambda i, j, k: (k, j)),
                  pl.BlockSpec((tm, tn), lambda i, j, k: (i, j)),
                  pl.BlockSpec((mr, tn), lambda i, j, k: (0, gate_off + j))],
        out_specs=pl.BlockSpec((tm, tn), lambda i, j, k: (i, j)),
        out_shape=jax.ShapeDtypeStruct((rows, n), F32),
        scratch_shapes=[pltpu.VMEM((tm, tn), F32)] if nk > 1 else [],
        compiler_params=_cparams(("arbitrary", "arbitrary", "arbitrary")),
        name="resid_mm",
    )(a, w, x, mod)


def _up_kernel(h_ref, wg_ref, wv_ref, prev_ref, wc_ref, bc_ref, u_ref, tail_ref, carry_scr, *, hp, shift):
    i = pl.program_id(1)

    @pl.when(i == 0)
    def _():
        carry_scr[...] = prev_ref[...]

    h = h_ref[...]
    g = jnp.dot(h, wg_ref[...], preferred_element_type=F32)
    v = jnp.dot(h, wv_ref[...], preferred_element_type=F32)
    tm = g.shape[0]
    gcat = jnp.concatenate([carry_scr[...], g], axis=0)
    wc = wc_ref[...]
    conv = (bc_ref[...]
            + wc[0:1, :] * gcat[hp - 2 * shift:hp - 2 * shift + tm, :]
            + wc[1:2, :] * gcat[hp - shift:hp - shift + tm, :]
            + wc[2:3, :] * g)
    u_ref[...] = (conv * jax.nn.sigmoid(conv) * v).astype(u_ref.dtype)
    tail = g[tm - hp:, :]
    carry_scr[...] = tail
    tail_ref[...] = tail


def _up(h2, w_up16, prev, w_conv, b_conv, tm, hp, shift):
    rows, d = h2.shape
    ff = w_conv.shape[1]
    tn = _pick(ff, (256, 128))
    nj = ff // tn
    kern = functools.partial(_up_kernel, hp=hp, shift=shift)
    return pl.pallas_call(
        kern,
        grid=(nj, rows // tm),
        in_specs=[pl.BlockSpec((tm, d), lambda j, i: (i, 0)),
                  pl.BlockSpec((d, tn), lambda j, i: (0, j)),
                  pl.BlockSpec((d, tn), lambda j, i: (0, nj + j)),
                  pl.BlockSpec((hp, tn), lambda j, i: (0, j)),
                  pl.BlockSpec((CONV_W, tn), lambda j, i: (0, j)),
                  pl.BlockSpec((1, tn), lambda j, i: (0, j))],
        out_specs=[pl.BlockSpec((tm, tn), lambda j, i: (i, j)),
                   pl.BlockSpec((hp, tn), lambda j, i: (0, j))],
        out_shape=[jax.ShapeDtypeStruct((rows, ff), BF16), jax.ShapeDtypeStruct((hp, ff), F32)],
        scratch_shapes=[pltpu.VMEM((hp, tn), F32)],
        compiler_params=_cparams(("arbitrary", "arbitrary")),
        name="ffn_up",
    )(h2, w_up16, w_up16, prev, w_conv, b_conv.reshape(1, ff))


def _rope_tables(pos):
    inv = ROPE_THETA ** (-jnp.arange(0, HEAD_DIM, 2, dtype=F32) / HEAD_DIM)
    ang = pos.astype(F32)[:, None] * inv[None, :]
    cos, sin = jnp.cos(ang), jnp.sin(ang)
    return jnp.concatenate([cos, cos], axis=1), jnp.concatenate([-sin, sin], axis=1)


def _layout(d, in_width):
    w2 = d // 2
    n_idx = (in_width - 6 * w2 - IDX_DIM) // (IDX_DIM + 1)
    qi = n_idx * IDX_DIM
    tn = min(512, w2)
    assert qi % w2 == 0 and w2 % tn == 0 and n_idx <= LANES
    lay = dict(w2=w2, n_idx=n_idx, tn=tn, iq=0, qa=qi, ka=qi + w2, va=qi + 2 * w2, qb=qi + 3 * w2,
               kb=qi + 4 * w2, vb=qi + 5 * w2, sp=qi + 6 * w2, total=qi + 6 * w2 + tn)
    return lay


def _cast_w_in(w, lay):
    w16 = w.astype(BF16)
    return w16, w16[:, w.shape[1] - 3 * lay["w2"]:]


def _layer(x2d, mod, pos, prev, attend, lw, lay, tm_rows, tm_norm, hp, shift):
    g_attn, w_in16, w_o16, g_ffn, w_up16, w_conv, b_conv, w_down16, g_final = lw
    rows, d = x2d.shape
    cosf, sinf = _rope_tables(pos)
    h = _norm_mod(x2d, g_attn, mod, 1, 0, tm_norm)
    p32, p16 = _proj(h, w_in16[0], w_in16[1], cosf, sinf, lay, tm_rows)
    o16 = attend(p32, p16)
    tn = _pick(d, (512, 256, 128))
    x1 = _resid_mm(o16, w_o16, x2d, mod, 2, tm_rows, tn, d)
    h2 = _norm_mod(x1, g_ffn, mod, 4, 3, tm_norm)
    u, tail = _up(h2, w_up16, prev, w_conv, b_conv, tm_rows, hp, shift)
    ff = u.shape[1]
    x2 = _resid_mm(u, w_down16, x1, mod, 5, min(tm_rows, 512), tn, ff)
    y = _final_norm(x2, g_final, tm_norm)
    return y, p32, tail


def kernel(x_prompt, x_sample, cache_k_a, cache_v_a, cache_k_idx, cache_k_b, cache_v_b, state_conv, page_table,
           c_prompt, c_sample, w_ada, b_ada, g_attn, w_in, lam_q1, lam_k1, lam_q2, lam_k2, g_subln, w_o, g_ffn,
           w_up, w_conv, b_conv, w_down, g_final):
    depth = w_ada.shape[0]
    assert depth == 1 and x_prompt.shape[0] == 1
    bsz, t_len, d = x_prompt.shape
    nb, s_len, _ = x_sample.shape
    n_pages = page_table.shape[1]
    past_len = n_pages * PAGE_SIZE
    ff = w_conv.shape[-1]
    lay = _layout(d, w_in.shape[-1])
    w2, n_idx = lay["w2"], lay["n_idx"]
    nha, nhb = w2 // HEAD_DIM, w2 // (2 * HEAD_DIM)
    l = 0
    lam_init = 0.8 - 0.6 * math.exp(-0.3 * l)

    lw = (g_attn[l], _cast_w_in(w_in[l], lay), w_o[l].astype(BF16), g_ffn[l], w_up[l].astype(BF16), w_conv[l], b_conv[l],
          w_down[l].astype(BF16), g_final)
    lams = tuple(a[l].reshape(1, HEAD_DIM) for a in (lam_q1, lam_k1, lam_q2, lam_k2))
    gs = g_subln[l].reshape(1, 2 * HEAD_DIM)

    n_c = nb + 16
    c_all = jnp.concatenate([c_sample, c_prompt, jnp.zeros((n_c - nb - 1, d), F32)], axis=0).astype(BF16)
    mod = _ada(c_all, w_ada[l], b_ada[l])
    mod_s, mod_p = mod[:nb], mod[nb:nb + 1]

    tq = _pick(t_len, (256, 128))
    tk = _pick(t_len, (512, 256, 128))
    k_sel_p = min(TOPK_MAX, t_len // 4)

    def attend_p(p32, p16):
        bias = _score_p(p16, p32, lay, tq, tk, k_sel_p)
        oa = _sattn_p(p16, bias, lay, tq, tk)
        ob = _dattn_p(p16, lams, gs, lay, tq, tk, lam_init)
        return jnp.concatenate([oa, ob], axis=1)

    tm_p = _pick(t_len, (1024, 512, 256, 128))
    y_p, p32_p, tail_p = _layer(x_prompt[0], mod_p, jnp.arange(t_len), jnp.zeros((8, ff), F32), attend_p, lw, lay,
                                tm_p, _pick(t_len, (512, 256, 128)), 8, 1)

    def seg(p32, name, width):
        return p32[:, lay[name]:lay[name] + width]

    outs_p = (
        y_p.reshape(1, t_len, d),
        seg(p32_p, "ka", w2).reshape(1, 1, t_len, nha, HEAD_DIM),
        seg(p32_p, "va", w2).reshape(1, 1, t_len, nha, HEAD_DIM),
        seg(p32_p, "sp", IDX_DIM).reshape(1, 1, t_len, IDX_DIM),
        seg(p32_p, "kb", w2).reshape(1, 1, t_len, nhb, 2, HEAD_DIM),
        seg(p32_p, "vb", w2).reshape(1, 1, t_len, nhb, 2 * HEAD_DIM),
        tail_p[8 - (CONV_W - 1):].reshape(1, 1, CONV_W - 1, ff),
    )

    rows_s = s_len * nb
    xs_tm = jnp.transpose(x_sample, (1, 0, 2)).reshape(rows_s, d)
    pos_s = past_len + jnp.repeat(jnp.arange(s_len), nb)
    prev_s = jnp.transpose(state_conv[l], (1, 0, 2)).reshape((CONV_W - 1) * nb, ff)
    k_sel_s = min(TOPK_MAX, (past_len + s_len) // 4)
    n_phys = cache_k_a.shape[1]

    def to_bm(a):
        return jnp.transpose(a.reshape(s_len, nb, a.shape[1]), (1, 0, 2))

    def pad8(a):
        return jnp.pad(a, ((0, 0), (0, 8 - s_len), (0, 0)))

    bm_cache = {}
    flat = lambda c: c.reshape(n_phys * PAGE_SIZE * nha, HEAD_DIM)

    def attend_s(p32, p16):
        bm = to_bm(p32)
        bm_cache["bm"] = bm
        sl = lambda name, width: bm[:, :, lay[name]:lay[name] + width]
        iq = sl("iq", n_idx * IDX_DIM).reshape(nb, s_len * n_idx, IDX_DIM)
        iw = bm[:, :, lay["sp"] + IDX_DIM:lay["sp"] + IDX_DIM + n_idx].reshape(nb, s_len * n_idx, 1)
        scores = _score_s(page_table, iq, iw, pad8(sl("sp", IDX_DIM)),
                          cache_k_idx[l].reshape(n_phys, PAGE_SIZE, IDX_DIM), past_len, s_len)
        l_pad = scores.shape[2]
        bias = _thresh(scores.reshape(nb * s_len, l_pad), k_sel_s).reshape(nb, s_len, l_pad)
        oa, ob = _attn_s(page_table, sl("qa", w2), sl("qb", w2), pad8(sl("ka", w2)), pad8(sl("va", w2)),
                         pad8(sl("kb", w2)), pad8(sl("vb", w2)), bias,
                         flat(cache_k_a[l]), flat(cache_v_a[l]), flat(cache_k_b[l]),
                         flat(jnp.swapaxes(cache_v_b[l].reshape(n_phys, PAGE_SIZE, nhb, 2, HEAD_DIM), 2, 3)),
                         lams, gs, lam_init)
        o = jnp.concatenate([oa, ob], axis=2)
        return jnp.transpose(o, (1, 0, 2)).reshape(rows_s, d).astype(BF16)

    y_s, _, tail_s = _layer(xs_tm, mod_s, pos_s, prev_s, attend_s, lw, lay, rows_s, nb, (CONV_W - 1) * nb, nb)
    bm = bm_cache["bm"]
    sb = lambda name, width: bm[:, :, lay[name]:lay[name] + width]
    outs_s = (
        jnp.transpose(y_s.reshape(s_len, nb, d), (1, 0, 2)),
        sb("ka", w2).reshape(1, nb, s_len, nha, HEAD_DIM),
        sb("va", w2).reshape(1, nb, s_len, nha, HEAD_DIM),
        sb("sp", IDX_DIM).reshape(1, nb, s_len, IDX_DIM),
        sb("kb", w2).reshape(1, nb, s_len, nhb, 2, HEAD_DIM),
        sb("vb", w2).reshape(1, nb, s_len, nhb, 2 * HEAD_DIM),
        jnp.transpose(tail_s.reshape(CONV_W - 1, nb, ff), (1, 0, 2)).reshape(1, nb, CONV_W - 1, ff),
    )
    return (outs_p[0], outs_s[0]) + outs_p[1:] + outs_s[1:]
```
